```python
import jax, jax.numpy as jnp
from jax import lax
import numpy as np

D_MODEL = 1024
BATCH = 8
SEQ = 2048
DEPTH = 4
DEC_BATCH = 128
DEC_SEQ = 8
PAST_LEN = 2048
PAGE_SIZE = 128

N_MIXERS = 3
N_META = 16
N_HEADS = 16
HEAD_DIM = D_MODEL // N_HEADS
Q_BLOCK = 128
CONV_WIDTH = 31
SSM_GROUP = 16
N_SSM_GROUPS = D_MODEL // SSM_GROUP
SSM_STATE = 64
D_FF = ((8 * D_MODEL // 3) + 127) // 128 * 128
N_SB_LAYERS = (DEPTH + 2) // 3
N_CONV_LAYERS = (DEPTH + 1) // 3
N_SSM_LAYERS = DEPTH // 3
EPS = 1e-6
DT_MIN = 1e-3
DT_MAX = 1e-1
SB_BIAS_INIT = -6.0

kernel_name = 'hybrid_stickbreak_conformer_s5_step'


def rmsnorm(x, g):
    xf = x.astype(jnp.float32)
    xf = xf * lax.rsqrt(jnp.mean(xf * xf, axis=-1, keepdims=True) + EPS)
    return (xf * g.astype(jnp.float32)).astype(x.dtype)


def swiglu_ffn(h, g, w_gate, w_up, w_down):
    n = rmsnorm(h, g)
    return (jax.nn.silu(n @ w_gate) * (n @ w_up)) @ w_down


def stick_breaking(q, k, v, bias, q_pos, k_pos, k_valid):
    z = jnp.einsum('bqhd,bkhd->bhqk', q.astype(jnp.float32), k.astype(jnp.float32)) * (HEAD_DIM ** -0.5)
    z = z + bias.astype(jnp.float32)[None, :, None, None]
    mask = (k_pos[None, :] < q_pos[:, None]) & k_valid[None, :]
    log_keep = jnp.where(mask, jax.nn.log_sigmoid(-z), 0.0)
    after = lax.cumsum(log_keep, axis=3, reverse=True) - log_keep
    w = jnp.where(mask, jnp.exp(jax.nn.log_sigmoid(z) + after), 0.0)
    return jnp.einsum('bhqk,bkhd->bqhd', w, v.astype(jnp.float32))


def sb_mixer(n, w_qkv, w_o, g_q, g_k, bias, k_past, v_past):
    b, t, _ = n.shape
    qkv = (n @ w_qkv).reshape(b, t, 3, N_HEADS, HEAD_DIM)
    q = rmsnorm(qkv[:, :, 0], g_q)
    k = rmsnorm(qkv[:, :, 1], g_k)
    v = qkv[:, :, 2]
    if k_past is None:
        pad = (-t) % Q_BLOCK
        lp = t + pad
        widths = ((0, 0), (pad, 0), (0, 0), (0, 0))
        qp, kp, vp = jnp.pad(q, widths), jnp.pad(k, widths), jnp.pad(v, widths)
        pos = jnp.arange(lp)
        valid = pos >= pad
        nb = lp // Q_BLOCK
        qb = qp.reshape(b, nb, Q_BLOCK, N_HEADS, HEAD_DIM).swapaxes(0, 1)
        pb = pos.reshape(nb, Q_BLOCK)
        ob = lax.map(lambda a: stick_breaking(a[0], kp, vp, bias, a[1], pos, valid), (qb, pb))
        o = ob.swapaxes(0, 1).reshape(b, lp, D_MODEL)[:, pad:]
    else:
        p_len = k_past.shape[1]
        k_all = jnp.concatenate([k_past.astype(k.dtype), k], axis=1)
        v_all = jnp.concatenate([v_past.astype(v.dtype), v], axis=1)
        k_pos = jnp.arange(p_len + t)
        q_pos = p_len + jnp.arange(t)
        o = stick_breaking(q, k_all, v_all, bias, q_pos, k_pos, k_pos >= 0).reshape(b, t, D_MODEL)
    return o.astype(n.dtype) @ w_o, k, v


def conv_module(n, state, w_pw1, w_dw, g_norm, w_pw2):
    a, gate = jnp.split(n @ w_pw1, 2, axis=-1)
    u = a * jax.nn.sigmoid(gate)
    ext = jnp.concatenate([state.astype(u.dtype), u], axis=1)
    y = lax.conv_general_dilated(ext, w_dw[:, None, :].astype(u.dtype), window_strides=(1,), padding='VALID',
                                 dimension_numbers=('NWC', 'WIO', 'NWC'), feature_group_count=D_MODEL)
    y = jax.nn.silu(rmsnorm(y, g_norm))
    return y @ w_pw2, ext[:, -(CONV_WIDTH - 1):]


def _cmul_combine(e1, e2):
    a1r, a1i, b1r, b1i = e1
    a2r, a2i, b2r, b2i = e2
    return (a2r * a1r - a2i * a1i, a2r * a1i + a2i * a1r,
            a2r * b1r - a2i * b1i + b2r, a2r * b1i + a2i * b1r + b2i)


def ssm_module(n, h0_re, h0_im, a_re, a_im, log_dt, b_re, b_im, c_re, c_im, d_skip, w_glu):
    bsz, t, _ = n.shape
    f32 = jnp.float32
    u = n.astype(f32).reshape(bsz, t, N_SSM_GROUPS, SSM_GROUP)
    a_re, a_im = a_re.astype(f32), a_im.astype(f32)
    dt = jnp.exp(log_dt.astype(f32))[:, None]
    mag = jnp.exp(dt * a_re)
    ab_re, ab_im = mag * jnp.cos(dt * a_im), mag * jnp.sin(dt * a_im)
    den = a_re * a_re + a_im * a_im
    nr, ni = ab_re - 1.0, ab_im
    f_re = (nr * a_re + ni * a_im) / den
    f_im = (ni * a_re - nr * a_im) / den
    b_re, b_im = b_re.astype(f32), b_im.astype(f32)
    bb_re = f_re[..., None] * b_re - f_im[..., None] * b_im
    bb_im = f_re[..., None] * b_im + f_im[..., None] * b_re
    bu_re = jnp.einsum('gpc,btgc->btgp', bb_re, u)
    bu_im = jnp.einsum('gpc,btgc->btgp', bb_im, u)
    h0_re, h0_im = h0_re.astype(f32), h0_im.astype(f32)
    bu_re = bu_re.at[:, 0].add(ab_re * h0_re - ab_im * h0_im)
    bu_im = bu_im.at[:, 0].add(ab_re * h0_im + ab_im * h0_re)
    ar = jnp.broadcast_to(ab_re, bu_re.shape)
    ai = jnp.broadcast_to(ab_im, bu_re.shape)
    _, _, h_re, h_im = lax.associative_scan(_cmul_combine, (ar, ai, bu_re, bu_im), axis=1)
    y = (jnp.einsum('gcp,btgp->btgc', c_re.astype(f32), h_re)
         - jnp.einsum('gcp,btgp->btgc', c_im.astype(f32), h_im)).reshape(bsz, t, D_MODEL)
    y = (y + d_skip.astype(f32) * n.astype(f32)).astype(n.dtype)
    za, zb = jnp.split(y @ w_glu, 2, axis=-1)
    return za * jax.nn.sigmoid(zb), h_re[:, -1], h_im[:, -1]


def trunk(h, paged, conv_state, ssm_re, ssm_im, p):
    new_k, new_v, new_conv, new_re, new_im = [], [], [], [], []
    for i in range(DEPTH):
        kind, j = i % N_MIXERS, i // N_MIXERS
        h = h + 0.5 * swiglu_ffn(h, p['norm_g'][i, 0], p['ffn1_w_gate'][i], p['ffn1_w_up'][i], p['ffn1_w_down'][i])
        n = rmsnorm(h, p['norm_g'][i, 1])
        if kind == 0:
            if paged is None:
                k_past, v_past = None, None
            else:
                cache_k, cache_v, page_table = paged
                bsz = page_table.shape[0]
                k_past = cache_k[j][page_table].reshape(bsz, -1, N_HEADS, HEAD_DIM)
                v_past = cache_v[j][page_table].reshape(bsz, -1, N_HEADS, HEAD_DIM)
            mix, k_rows, v_rows = sb_mixer(n, p['sb_w_qkv'][j], p['sb_w_o'][j], p['sb_q_norm'][j],
                                           p['sb_k_norm'][j], p['sb_logit_bias'][j], k_past, v_past)
            new_k.append(k_rows)
            new_v.append(v_rows)
        elif kind == 1:
            mix, cs = conv_module(n, conv_state[j], p['conv_w_pw1'][j], p['conv_w_dw'][j],
                                  p['conv_norm_g'][j], p['conv_w_pw2'][j])
            new_conv.append(cs)
        else:
            mix, sr, si = ssm_module(n, ssm_re[j], ssm_im[j], p['ssm_A_re'][j], p['ssm_A_im'][j],
                                     p['ssm_log_dt'][j], p['ssm_B_re'][j], p['ssm_B_im'][j],
                                     p['ssm_C_re'][j], p['ssm_C_im'][j], p['ssm_D'][j], p['ssm_w_glu'][j])
            new_re.append(sr)
            new_im.append(si)
        h = h + mix.astype(h.dtype)
        h = h + 0.5 * swiglu_ffn(h, p['norm_g'][i, 2], p['ffn2_w_gate'][i], p['ffn2_w_up'][i], p['ffn2_w_down'][i])
    return h, jnp.stack(new_k), jnp.stack(new_v), jnp.stack(new_conv), jnp.stack(new_re), jnp.stack(new_im)


def setup_inputs(seed: int = 0) -> dict:
    key = jax.random.key(seed)
    ks = jax.random.split(key, 40)
    f32 = jnp.float32

    def nrm(k, shape, scale=1.0):
        return jax.random.normal(k, shape, f32) * scale

    n_pages = PAST_LEN // PAGE_SIZE
    n_used = DEC_BATCH * n_pages
    n_pool = n_used + (n_used + 3) // 4
    page_table = jax.random.permutation(ks[4], n_pool)[:n_used].reshape(DEC_BATCH, n_pages).astype(jnp.int32)
    G, P, GC = N_SSM_GROUPS, SSM_STATE, SSM_GROUP
    return {
        'x_prompt': nrm(ks[0], (BATCH, SEQ, D_MODEL)),
        'x_sample': nrm(ks[1], (DEC_BATCH, DEC_SEQ, D_MODEL)),
        'cache_k': nrm(ks[2], (N_SB_LAYERS, n_pool, PAGE_SIZE, N_HEADS, HEAD_DIM)),
        'cache_v': nrm(ks[3], (N_SB_LAYERS, n_pool, PAGE_SIZE, N_HEADS, HEAD_DIM)),
        'page_table': page_table,
        'state_conv': nrm(ks[5], (N_CONV_LAYERS, DEC_BATCH, CONV_WIDTH - 1, D_MODEL), 0.5),
        'state_ssm_re': nrm(ks[6], (N_SSM_LAYERS, DEC_BATCH, G, P), 0.5),
        'state_ssm_im': nrm(ks[7], (N_SSM_LAYERS, DEC_BATCH, G, P), 0.5),
        'meta_tokens': nrm(ks[8], (N_META, D_MODEL)),
        'norm_g': 1.0 + nrm(ks[9], (DEPTH, 3, D_MODEL), 0.02),
        'ffn1_w_gate': nrm(ks[10], (DEPTH, D_MODEL, D_FF), D_MODEL ** -0.5),
        'ffn1_w_up': nrm(ks[11], (DEPTH, D_MODEL, D_FF), D_MODEL ** -0.5),
        'ffn1_w_down': nrm(ks[12], (DEPTH, D_FF, D_MODEL), D_FF ** -0.5),
        'ffn2_w_gate': nrm(ks[13], (DEPTH, D_MODEL, D_FF), D_MODEL ** -0.5),
        'ffn2_w_up': nrm(ks[14], (DEPTH, D_MODEL, D_FF), D_MODEL ** -0.5),
        'ffn2_w_down': nrm(ks[15], (DEPTH, D_FF, D_MODEL), D_FF ** -0.5),
        'sb_w_qkv': nrm(ks[16], (N_SB_LAYERS, D_MODEL, 3 * D_MODEL), D_MODEL ** -0.5),
        'sb_w_o': nrm(ks[17], (N_SB_LAYERS, D_MODEL, D_MODEL), D_MODEL ** -0.5),
        'sb_q_norm': 1.0 + nrm(ks[18], (N_SB_LAYERS, HEAD_DIM), 0.02),
        'sb_k_norm': 1.0 + nrm(ks[19], (N_SB_LAYERS, HEAD_DIM), 0.02),
        'sb_logit_bias': SB_BIAS_INIT + nrm(ks[33], (N_SB_LAYERS, N_HEADS), 0.1),
        'conv_w_pw1': nrm(ks[20], (N_CONV_LAYERS, D_MODEL, 2 * D_MODEL), D_MODEL ** -0.5),
        'conv_w_dw': nrm(ks[21], (N_CONV_LAYERS, CONV_WIDTH, D_MODEL), CONV_WIDTH ** -0.5),
        'conv_norm_g': 1.0 + nrm(ks[22], (N_CONV_LAYERS, D_MODEL), 0.02),
        'conv_w_pw2': nrm(ks[23], (N_CONV_LAYERS, D_MODEL, D_MODEL), D_MODEL ** -0.5),
        'ssm_A_re': -0.5 + nrm(ks[24], (N_SSM_LAYERS, G, P), 0.01),
        'ssm_A_im': jnp.pi * jnp.arange(P, dtype=f32) + nrm(ks[25], (N_SSM_LAYERS, G, P), 0.01),
        'ssm_log_dt': jax.random.uniform(ks[26], (N_SSM_LAYERS, G), f32, np.log(DT_MIN), np.log(DT_MAX)),
        'ssm_B_re': nrm(ks[27], (N_SSM_LAYERS, G, P, GC), (2 * GC) ** -0.5),
        'ssm_B_im': nrm(ks[28], (N_SSM_LAYERS, G, P, GC), (2 * GC) ** -0.5),
        'ssm_C_re': nrm(ks[29], (N_SSM_LAYERS, G, GC, P), P ** -0.5),
        'ssm_C_im': nrm(ks[30], (N_SSM_LAYERS, G, GC, P), P ** -0.5),
        'ssm_D': nrm(ks[31], (N_SSM_LAYERS, D_MODEL)),
        'ssm_w_glu': nrm(ks[32], (N_SSM_LAYERS, D_MODEL, 2 * D_MODEL), D_MODEL ** -0.5),
    }


def reference(x_prompt, x_sample, cache_k, cache_v, page_table, state_conv, state_ssm_re, state_ssm_im,
              meta_tokens, norm_g, ffn1_w_gate, ffn1_w_up, ffn1_w_down, ffn2_w_gate, ffn2_w_up, ffn2_w_down,
              sb_w_qkv, sb_w_o, sb_q_norm, sb_k_norm, sb_logit_bias, conv_w_pw1, conv_w_dw, conv_norm_g, conv_w_pw2,
              ssm_A_re, ssm_A_im, ssm_log_dt, ssm_B_re, ssm_B_im, ssm_C_re, ssm_C_im, ssm_D, ssm_w_glu):
    p = {'norm_g': norm_g, 'ffn1_w_gate': ffn1_w_gate, 'ffn1_w_up': ffn1_w_up, 'ffn1_w_down': ffn1_w_down,
         'ffn2_w_gate': ffn2_w_gate, 'ffn2_w_up': ffn2_w_up, 'ffn2_w_down': ffn2_w_down,
         'sb_w_qkv': sb_w_qkv, 'sb_w_o': sb_w_o, 'sb_q_norm': sb_q_norm, 'sb_k_norm': sb_k_norm,
         'sb_logit_bias': sb_logit_bias,
         'conv_w_pw1': conv_w_pw1, 'conv_w_dw': conv_w_dw, 'conv_norm_g': conv_norm_g, 'conv_w_pw2': conv_w_pw2,
         'ssm_A_re': ssm_A_re, 'ssm_A_im': ssm_A_im, 'ssm_log_dt': ssm_log_dt, 'ssm_B_re': ssm_B_re,
         'ssm_B_im': ssm_B_im, 'ssm_C_re': ssm_C_re, 'ssm_C_im': ssm_C_im, 'ssm_D': ssm_D, 'ssm_w_glu': ssm_w_glu}

    b = x_prompt.shape[0]
    meta = jnp.broadcast_to(meta_tokens.astype(x_prompt.dtype), (b, N_META, D_MODEL))
    hp = jnp.concatenate([meta, x_prompt], axis=1)
    conv0 = jnp.zeros((N_CONV_LAYERS, b, CONV_WIDTH - 1, D_MODEL), x_prompt.dtype)
    ssm0 = jnp.zeros((N_SSM_LAYERS, b, N_SSM_GROUPS, SSM_STATE), jnp.float32)
    hp, k_prompt, v_prompt, conv_prompt, ssm_re_prompt, ssm_im_prompt = trunk(hp, None, conv0, ssm0, ssm0, p)
    y_prompt = hp[:, N_META:]

    hs, k_sample, v_sample, conv_sample, ssm_re_sample, ssm_im_sample = trunk(
        x_sample, (cache_k, cache_v, page_table), state_conv, state_ssm_re, state_ssm_im, p)
    y_sample = hs

    return (y_prompt, y_sample, k_prompt, v_prompt, k_sample, v_sample, conv_prompt, conv_sample,
            ssm_re_prompt, ssm_im_prompt, ssm_re_sample, ssm_im_sample)
```

```python
import functools

import jax
import jax.numpy as jnp
from jax import lax
from jax.experimental import pallas as pl
from jax.experimental.pallas import tpu as pltpu

F32 = jnp.float32
BF16 = jnp.bfloat16
EPS = 1e-6
LANES = 128
SUBLANES = 8
VMEM_LIMIT = 56 * 1024 * 1024
SEQ_TILE = 128
SSM_LANE_CHUNK = 512


def _round_up(x, m):
    return (x + m - 1) // m * m


def _largest_tile(n, cap, mult=SUBLANES):
    best = mult
    for t in range(mult, cap + 1, mult):
        if n % t == 0:
            best = t
    return best


def _params(n_grid):
    return pltpu.CompilerParams(dimension_semantics=("arbitrary",) * n_grid,
                                vmem_limit_bytes=VMEM_LIMIT)


def _const_spec(shape):
    nd = len(shape)
    return pl.BlockSpec(shape, lambda *_: (0,) * nd, pipeline_mode=pl.Buffered(1))


def _dot(a, b):
    return jnp.dot(a, b, preferred_element_type=F32)


def _dot_t(a, b):
    return lax.dot_general(a, b, (((1,), (1,)), ((), ())), preferred_element_type=F32)


def _rms(x, g):
    ms = jnp.mean(x * x, axis=-1, keepdims=True)
    return x * lax.rsqrt(ms + EPS) * g


def _sigmoid(x):
    return 1.0 / (1.0 + jnp.exp(-x))


def _ffn_kernel(h_ref, g_ref, wg_ref, wu_ref, wd_ref, o_ref, *, fc):
    h = h_ref[...]
    n = _rms(h, g_ref[...]).astype(BF16)
    acc = jnp.zeros(h.shape, F32)
    for c in range(wg_ref.shape[1] // fc):
        sl = slice(c * fc, (c + 1) * fc)
        gate = _dot(n, wg_ref[:, sl])
        up = _dot(n, wu_ref[:, sl])
        a = (gate * _sigmoid(gate) * up).astype(BF16)
        acc = acc + _dot(a, wd_ref[sl, :])
    o_ref[...] = h + 0.5 * acc


def _ffn(h, g, wg, wu, wd, tm):
    m, d = h.shape
    f = wg.shape[1]
    fc = 256 if f % 256 == 0 else LANES
    row = pl.BlockSpec((tm, d), lambda i: (i, 0))
    return pl.pallas_call(
        functools.partial(_ffn_kernel, fc=fc),
        grid=(m // tm,),
        in_specs=[row, _const_spec((1, d)), _const_spec((d, f)), _const_spec((d, f)),
                  _const_spec((f, d))],
        out_specs=row,
        out_shape=jax.ShapeDtypeStruct((m, d), F32),
        input_output_aliases={0: 0},
        compiler_params=_params(1),
        name="ffn",
    )(h, g, wg, wu, wd)


def _proj_res_kernel(h_ref, x_ref, w_ref, o_ref):
    o_ref[...] = h_ref[...] + _dot(x_ref[...].astype(BF16), w_ref[...])


def _proj_res(h, x, w, tm):
    m, d = h.shape
    row = pl.BlockSpec((tm, d), lambda i: (i, 0))
    return pl.pallas_call(
        _proj_res_kernel,
        grid=(m // tm,),
        in_specs=[row, pl.BlockSpec((tm, x.shape[1]), lambda i: (i, 0)), _const_spec(w.shape)],
        out_specs=row,
        out_shape=jax.ShapeDtypeStruct((m, d), F32),
        input_output_aliases={0: 0},
        compiler_params=_params(1),
        name="proj_res",
    )(h, x, w)


def _norm_glu_kernel(h_ref, g_ref, w_ref, o_ref):
    d = h_ref.shape[1]
    n = _rms(h_ref[...], g_ref[...]).astype(BF16)
    a = _dot(n, w_ref[:, :d])
    gate = _dot(n, w_ref[:, d:])
    o_ref[...] = a * _sigmoid(gate)


def _norm_glu(h, g, w, tm):
    m, d = h.shape
    row = pl.BlockSpec((tm, d), lambda i: (i, 0))
    return pl.pallas_call(
        _norm_glu_kernel,
        grid=(m // tm,),
        in_specs=[row, _const_spec((1, d)), _const_spec(w.shape)],
        out_specs=row,
        out_shape=jax.ShapeDtypeStruct((m, d), F32),
        compiler_params=_params(1),
        name="norm_glu",
    )(h, g, w)


def _qkv_kernel(h_ref, g_ref, w_ref, gq_ref, gk_ref, hm_ref, q_ref, k_ref, v_ref, *, hd):
    d = h_ref.shape[1]
    n = _rms(h_ref[...], g_ref[...]).astype(BF16)
    hm = hm_ref[...]

    def head_norm(x, gain):
        x2 = (x * x).astype(BF16)
        ms = jnp.concatenate(
            [_dot(x2[:, c * LANES:(c + 1) * LANES], hm) for c in range(d // LANES)], axis=1)
        return x * lax.rsqrt(ms * (1.0 / hd) + EPS) * gain

    q = head_norm(_dot(n, w_ref[:, :d]), gq_ref[...])
    q_ref[...] = q * (hd ** -0.5)
    k_ref[...] = head_norm(_dot(n, w_ref[:, d:2 * d]), gk_ref[...])
    v_ref[...] = _dot(n, w_ref[:, 2 * d:])


def _qkv(h, g, w, gq, gk, hm, hd, tm):
    m, d = h.shape
    row = pl.BlockSpec((tm, d), lambda i: (i, 0))
    out = jax.ShapeDtypeStruct((m, d), F32)
    return pl.pallas_call(
        functools.partial(_qkv_kernel, hd=hd),
        grid=(m // tm,),
        in_specs=[row, _const_spec((1, d)), _const_spec(w.shape), _const_spec((1, d)),
                  _const_spec((1, d)), _const_spec(hm.shape)],
        out_specs=[row, row, row],
        out_shape=[out, out, out],
        compiler_params=_params(1),
        name="qkv",
    )(h, g, w, gq, gk, hm)


def _sb_tile(z, mask, carry, tri_ones):
    tk = z.shape[1]
    lk = -(jnp.maximum(z, 0.0) + jnp.log(1.0 + jnp.exp(-jnp.abs(z))))
    if mask is not None:
        lk = jnp.where(mask, lk, 0.0)
    hi = lk.astype(BF16)
    lo = (lk - hi.astype(F32)).astype(BF16)
    s = _dot(hi, tri_ones) + _dot(lo, tri_ones)
    w = jnp.exp(z + lk + s[:, :tk] + carry)
    if mask is not None:
        w = jnp.where(mask, w, 0.0)
    return w, carry + s[:, tk:]


def _attn_prompt_kernel(bias_ref, q_ref, k_ref, v_ref, to_ref, o_ref, *, hd, ts):
    hp = pl.program_id(1)
    qi = pl.program_id(2)
    hpl = LANES // hd
    q = q_ref[...]
    lane = lax.broadcasted_iota(jnp.int32, (ts, LANES), 1)
    row = lax.broadcasted_iota(jnp.int32, (ts, ts), 0)
    col = lax.broadcasted_iota(jnp.int32, (ts, ts), 1)
    causal = col < row
    tri_ones = to_ref[...]
    qh = [jnp.where(lane // hd == j, q, 0.0).astype(BF16) for j in range(hpl)]
    bias = [bias_ref[hp * hpl + j] for j in range(hpl)]

    def tile(kt, state, mask):
        k0 = pl.multiple_of(kt * ts, ts)
        kb = k_ref[pl.ds(k0, ts), :].astype(BF16)
        vb = v_ref[pl.ds(k0, ts), :].astype(BF16)
        new = []
        for j in range(hpl):
            carry, acc = state[j]
            w, carry = _sb_tile(_dot_t(qh[j], kb) + bias[j], mask, carry, tri_ones)
            new.append((carry, acc + _dot(w.astype(BF16), vb)))
        return tuple(new)

    zero = tuple((jnp.zeros((ts, ts), F32), jnp.zeros((ts, LANES), F32)) for _ in range(hpl))
    state = tile(qi, zero, causal)
    state = lax.fori_loop(0, qi, lambda i, st: tile(qi - 1 - i, st, None), state)
    o = state[0][1]
    for j in range(1, hpl):
        o = jnp.where(lane // hd == j, state[j][1], o)
    o_ref[...] = o


def _attn_prompt(q, k, v, bias, tri_ones, nb, t_pad, hd):
    m, d = q.shape
    ts = SEQ_TILE
    nqt = t_pad // ts
    seq = pl.BlockSpec((t_pad, LANES), lambda b, hp, qi: (b, hp))
    tile = pl.BlockSpec((ts, LANES), lambda b, hp, qi: (b * nqt + qi, hp))
    return pl.pallas_call(
        functools.partial(_attn_prompt_kernel, hd=hd, ts=ts),
        grid=(nb, d // LANES, nqt),
        in_specs=[pl.BlockSpec(memory_space=pltpu.SMEM), tile, seq, seq,
                  _const_spec(tri_ones.shape)],
        out_specs=tile,
        out_shape=jax.ShapeDtypeStruct((m, d), F32),
        compiler_params=_params(3),
        name="attn_prompt",
    )(bias, q, k, v, tri_ones)


def _attn_sample_kernel(pt_ref, q_ref, kn_ref, vn_ref, kc_ref, vc_ref, bias_ref, to_ref,
                        o_in_ref, o_ref, qbd_ref, carry_ref, acc_ref, *, nh, hd, n_pages):
    del pt_ref, o_in_ref
    s = pl.program_id(1)
    tq, d = q_ref.shape
    tk = kc_ref.shape[0]
    r = nh * tq
    tri_ones = to_ref[...]

    @pl.when(s == 0)
    def _():
        row_h = lax.broadcasted_iota(jnp.int32, (r, d), 0) // tq
        lane_h = lax.broadcasted_iota(jnp.int32, (r, d), 1) // hd
        qbd = jnp.where(row_h == lane_h, jnp.tile(q_ref[...], (nh, 1)), 0.0).astype(BF16)
        qbd_ref[...] = qbd
        pad = jnp.zeros((tk - tq, d), F32)
        kb = jnp.concatenate([kn_ref[...], pad], axis=0).astype(BF16)
        vb = jnp.concatenate([vn_ref[...], pad], axis=0).astype(BF16)
        qpos = lax.broadcasted_iota(jnp.int32, (r, tk), 0) % tq
        kpos = lax.broadcasted_iota(jnp.int32, (r, tk), 1)
        w, carry = _sb_tile(_dot_t(qbd, kb) + bias_ref[...], kpos < qpos,
                            jnp.zeros((r, tk), F32), tri_ones)
        carry_ref[...] = carry
        acc_ref[...] = _dot(w.astype(BF16), vb)

    @pl.when(s > 0)
    def _():
        kb = kc_ref[...].astype(BF16)
        w, carry = _sb_tile(_dot_t(qbd_ref[...], kb) + bias_ref[...], None, carry_ref[...],
                            tri_ones)
        carry_ref[...] = carry
        acc_ref[...] += _dot(w.astype(BF16), vc_ref[...].astype(BF16))

    @pl.when(s == n_pages)
    def _():
        lane_h = lax.broadcasted_iota(jnp.int32, (tq, d), 1) // hd
        o = acc_ref[0:tq, :]
        for j in range(1, nh):
            o = jnp.where(lane_h == j, acc_ref[j * tq:(j + 1) * tq, :], o)
        o_ref[...] = o


def _attn_sample(o, q, k, v, cache_k, cache_v, layer, page_table, bias_rows, tri_ones,
                 m_p, nb, tq, nh, hd):
    m, d = q.shape
    n_pages = page_table.shape[1]
    tk = cache_k.shape[2]
    r = nh * tq
    new_rows = pl.BlockSpec((tq, d), lambda b, s, pt: (m_p // tq + b, 0))

    def page_map(b, s, pt):
        return (layer, pt[b * n_pages + n_pages - jnp.maximum(s, 1)], 0, 0)

    page = pl.BlockSpec((None, None, tk, d), page_map)
    grid_spec = pltpu.PrefetchScalarGridSpec(
        num_scalar_prefetch=1,
        grid=(nb, n_pages + 1),
        in_specs=[new_rows, new_rows, new_rows, page, page, _const_spec(bias_rows.shape),
                  _const_spec(tri_ones.shape), pl.BlockSpec(memory_space=pl.ANY)],
        out_specs=new_rows,
        scratch_shapes=[pltpu.VMEM((r, d), BF16), pltpu.VMEM((r, tk), F32),
                        pltpu.VMEM((r, d), F32)],
    )
    return pl.pallas_call(
        functools.partial(_attn_sample_kernel, nh=nh, hd=hd, n_pages=n_pages),
        grid_spec=grid_spec,
        out_shape=jax.ShapeDtypeStruct((m, d), F32),
        input_output_aliases={8: 0},
        compiler_params=_params(2),
        name="attn_sample",
    )(page_table.reshape(-1), q, k, v, cache_k, cache_v, bias_rows, tri_ones, o)


def _conv_prompt_kernel(u_ref, h_ref, wdw_ref, gn_ref, w2_ref, o_ref, ext_ref, *, ts, halo):
    ti = pl.program_id(1)
    width = wdw_ref.shape[0]

    @pl.when(ti == 0)
    def _():
        ext_ref[0:halo, :] = jnp.zeros((halo, ext_ref.shape[1]), F32)
        ext_ref[halo:, :] = u_ref[...]

    t0 = pl.multiple_of(ti * ts, ts)
    win = ext_ref[pl.ds(t0, ts + halo), :]
    off = halo - (width - 1)
    y = jnp.zeros((ts, win.shape[1]), F32)
    for j in range(width):
        y = y + win[off + j:off + j + ts, :] * wdw_ref[j:j + 1, :]
    yn = _rms(y, gn_ref[...])
    o_ref[...] = h_ref[...] + _dot((yn * _sigmoid(yn)).astype(BF16), w2_ref[...])


def _conv_prompt(h, u, wdw, gn, w2, nb, t_pad):
    m, d = h.shape
    ts = SEQ_TILE
    nt = t_pad // ts
    halo = _round_up(wdw.shape[0] - 1, SUBLANES)
    tile = pl.BlockSpec((ts, d), lambda b, t: (b * nt + t, 0))
    return pl.pallas_call(
        functools.partial(_conv_prompt_kernel, ts=ts, halo=halo),
        grid=(nb, nt),
        in_specs=[pl.BlockSpec((t_pad, d), lambda b, t: (b, 0)), tile, _const_spec(wdw.shape),
                  _const_spec((1, d)), _const_spec(w2.shape)],
        out_specs=tile,
        out_shape=jax.ShapeDtypeStruct((m, d), F32),
        input_output_aliases={1: 0},
        scratch_shapes=[pltpu.VMEM((halo + t_pad, d), F32)],
        compiler_params=_params(2),
        name="conv_prompt",
    )(u, h, wdw, gn, w2)


def _conv_sample_kernel(st_ref, u_ref, h_ref, wdw_ref, gn_ref, w2_ref, o_ref):
    width = wdw_ref.shape[0]
    ns = st_ref.shape[0]
    for t in range(u_ref.shape[0]):
        y = jnp.zeros(h_ref.shape[1:], F32)
        for j in range(width):
            i = t + j
            x = st_ref[i] if i < ns else u_ref[i - ns]
            y = y + x * wdw_ref[j:j + 1, :]
        yn = _rms(y, gn_ref[...])
        o_ref[t] = h_ref[t] + _dot((yn * _sigmoid(yn)).astype(BF16), w2_ref[...])


def _conv_sample(h_t, u_t, st_t, wdw, gn, w2):
    t, nb, d = h_t.shape
    bs = _largest_tile(nb, 32)
    blk = lambda n: pl.BlockSpec((n, bs, d), lambda i: (0, i, 0))
    return pl.pallas_call(
        _conv_sample_kernel,
        grid=(nb // bs,),
        in_specs=[blk(st_t.shape[0]), blk(t), blk(t), _const_spec(wdw.shape),
                  _const_spec((1, d)), _const_spec(w2.shape)],
        out_specs=blk(t),
        out_shape=jax.ShapeDtypeStruct((t, nb, d), F32),
        compiler_params=_params(1),
        name="conv_sample",
    )(st_t, u_t, h_t, wdw, gn, w2)


def _ssm_prep_kernel(are_ref, aim_ref, ldt_ref, bre_ref, bim_ref,
                     pwre_ref, pwim_ref, bbre_ref, bbim_ref):
    a_re, a_im = are_ref[...], aim_ref[...]
    dt = jnp.exp(ldt_ref[...])
    mag = jnp.exp(dt * a_re)
    ab_re, ab_im = mag * jnp.cos(dt * a_im), mag * jnp.sin(dt * a_im)
    den = a_re * a_re + a_im * a_im
    nr, ni = ab_re - 1.0, ab_im
    f_re = (nr * a_re + ni * a_im) / den
    f_im = (ni * a_re - nr * a_im) / den
    b_re, b_im = bre_ref[...], bim_ref[...]
    bbre_ref[...] = f_re * b_re - f_im * b_im
    bbim_ref[...] = f_re * b_im + f_im * b_re
    p_re, p_im = ab_re, ab_im
    for k in range(pwre_ref.shape[0]):
        pwre_ref[k:k + 1, :] = p_re
        pwim_ref[k:k + 1, :] = p_im
        p_re, p_im = p_re * ab_re - p_im * ab_im, p_re * ab_im + p_im * ab_re


def _ssm_prep(a_re, a_im, ldt, b_re, b_im):
    gp = a_re.shape[1]
    gc = b_re.shape[0]
    vec = jax.ShapeDtypeStruct((SUBLANES, gp), F32)
    mat = jax.ShapeDtypeStruct((gc, gp), F32)
    return pl.pallas_call(
        _ssm_prep_kernel,
        out_shape=[vec, vec, mat, mat],
        compiler_params=pltpu.CompilerParams(vmem_limit_bytes=VMEM_LIMIT),
        name="ssm_prep",
    )(a_re, a_im, ldt, b_re, b_im)


def _ssm_kernel(*refs, chained, t_last, lc):
    if chained:
        (h_ref, g_ref, bbre_ref, bbim_ref, ccre_ref, ccim_ref, pwre_ref, pwim_ref, dsk_ref,
         wglu_ref, o_ref, stre_ref, stim_ref, sre, sim, cre, cim) = refs
    else:
        (h_ref, g_ref, bbre_ref, bbim_ref, ccre_ref, ccim_ref, pwre_ref, pwim_ref, dsk_ref,
         wglu_ref, h0re_ref, h0im_ref, o_ref, stre_ref, stim_ref, sre, sim) = refs
    rows, d = h_ref.shape
    gp = sre.shape[1]
    nblk = d // LANES
    cw = gp // nblk

    h = h_ref[...]
    n = _rms(h, g_ref[...])
    nb16 = n.astype(BF16)
    for c in range(nblk):
        x = nb16[:, c * LANES:(c + 1) * LANES]
        sre[:, c * cw:(c + 1) * cw] = _dot(x, bbre_ref[c])
        sim[:, c * cw:(c + 1) * cw] = _dot(x, bbim_ref[c])

    if chained:
        ti = pl.program_id(1)

        @pl.when(ti == 0)
        def _():
            cre[...] = jnp.zeros(cre.shape, F32)
            cim[...] = jnp.zeros(cim.shape, F32)

    rowi = lax.broadcasted_iota(jnp.int32, (SUBLANES, lc), 0)
    for c in range(gp // lc):
        ls = slice(c * lc, (c + 1) * lc)
        p_re, p_im = pwre_ref[:, ls], pwim_ref[:, ls]

        def body(r, carry, ls=ls, p_re=p_re, p_im=p_im):
            r0 = pl.multiple_of(r * SUBLANES, SUBLANES)
            if chained:
                c_re, c_im = carry
            else:
                c_re, c_im = h0re_ref[pl.ds(r, 1), ls], h0im_ref[pl.ds(r, 1), ls]
            x_re, x_im = sre[pl.ds(r0, SUBLANES), ls], sim[pl.ds(r0, SUBLANES), ls]
            for dist in (1, 2, 4):
                a_re, a_im = p_re[dist - 1:dist], p_im[dist - 1:dist]
                s_re = jnp.where(rowi >= dist, pltpu.roll(x_re, dist, axis=0), 0.0)
                s_im = jnp.where(rowi >= dist, pltpu.roll(x_im, dist, axis=0), 0.0)
                x_re, x_im = (x_re + a_re * s_re - a_im * s_im,
                              x_im + a_re * s_im + a_im * s_re)
            x_re, x_im = (x_re + p_re * c_re - p_im * c_im,
                          x_im + p_re * c_im + p_im * c_re)
            sre[pl.ds(r0, SUBLANES), ls] = x_re
            sim[pl.ds(r0, SUBLANES), ls] = x_im
            if chained:
                return x_re[SUBLANES - 1:], x_im[SUBLANES - 1:]
            stre_ref[pl.ds(r, 1), ls] = x_re[SUBLANES - 1:]
            stim_ref[pl.ds(r, 1), ls] = x_im[SUBLANES - 1:]
            return carry

        if chained:
            c_re, c_im = lax.fori_loop(0, rows // SUBLANES, body, (cre[:, ls], cim[:, ls]))
            cre[:, ls] = c_re
            cim[:, ls] = c_im
        else:
            lax.fori_loop(0, rows // SUBLANES, body, 0)

    if chained:
        @pl.when(ti == t_last // rows)
        def _():
            stre_ref[0] = sre[t_last % rows:t_last % rows + 1, :]
            stim_ref[0] = sim[t_last % rows:t_last % rows + 1, :]

    hre = sre[...].astype(BF16)
    him = sim[...].astype(BF16)
    y = jnp.concatenate(
        [_dot(hre[:, c * cw:(c + 1) * cw], ccre_ref[c]) - _dot(him[:, c * cw:(c + 1) * cw], ccim_ref[c])
         for c in range(nblk)], axis=1)
    y = (y + dsk_ref[...] * n).astype(BF16)
    za = _dot(y, wglu_ref[:, :d])
    zb = _dot(y, wglu_ref[:, d:])
    o_ref[...] = h + za * _sigmoid(zb)


def _ssm_common_specs(d, bb, cc, pw, wglu):
    return [_const_spec((1, d)), _const_spec(bb.shape), _const_spec(bb.shape),
            _const_spec(cc.shape), _const_spec(cc.shape), _const_spec(pw.shape),
            _const_spec(pw.shape), _const_spec((1, d)), _const_spec(wglu.shape)]


def _ssm_prompt(h, g, bbre, bbim, ccre, ccim, pwre, pwim, dsk, wglu, nb, t_pad, t_valid):
    m, d = h.shape
    gp = pwre.shape[1]
    tt = _largest_tile(t_pad, 256)
    nt = t_pad // tt
    lc = min(SSM_LANE_CHUNK, gp)
    tile = pl.BlockSpec((tt, d), lambda b, t: (b * nt + t, 0))
    st = pl.BlockSpec((1, 1, gp), lambda b, t: (b, 0, 0))
    st_shape = jax.ShapeDtypeStruct((nb, 1, gp), F32)
    return pl.pallas_call(
        functools.partial(_ssm_kernel, chained=True, t_last=t_valid - 1, lc=lc),
        grid=(nb, nt),
        in_specs=[tile] + _ssm_common_specs(d, bbre, ccre, pwre, wglu),
        out_specs=[tile, st, st],
        out_shape=[jax.ShapeDtypeStruct((m, d), F32), st_shape, st_shape],
        input_output_aliases={0: 0},
        scratch_shapes=[pltpu.VMEM((tt, gp), F32), pltpu.VMEM((tt, gp), F32),
                        pltpu.VMEM((1, gp), F32), pltpu.VMEM((1, gp), F32)],
        compiler_params=_params(2),
        name="ssm_prompt",
    )(h, g, bbre, bbim, ccre, ccim, pwre, pwim, dsk, wglu)


def _ssm_sample(h, g, bbre, bbim, ccre, ccim, pwre, pwim, dsk, wglu, h0re, h0im, m_p, t):
    m, d = h.shape
    assert t == SUBLANES, "the sample scan handles one 8-token block per sequence"
    nb, gp = h0re.shape
    bs = _largest_tile(nb, 32)
    rows = bs * t
    lc = min(SSM_LANE_CHUNK, gp)
    tile = pl.BlockSpec((rows, d), lambda i: (m_p // rows + i, 0))
    st = pl.BlockSpec((bs, gp), lambda i: (i, 0))
    st_shape = jax.ShapeDtypeStruct((nb, gp), F32)
    return pl.pallas_call(
        functools.partial(_ssm_kernel, chained=False, t_last=0, lc=lc),
        grid=(nb // bs,),
        in_specs=[tile] + _ssm_common_specs(d, bbre, ccre, pwre, wglu) + [st, st],
        out_specs=[tile, st, st],
        out_shape=[jax.ShapeDtypeStruct((m, d), F32), st_shape, st_shape],
        input_output_aliases={0: 0},
        scratch_shapes=[pltpu.VMEM((rows, gp), F32), pltpu.VMEM((rows, gp), F32)],
        compiler_params=_params(1),
        name="ssm_sample",
    )(h, g, bbre, bbim, ccre, ccim, pwre, pwim, dsk, wglu, h0re, h0im)


def _block_diag_blocks(full, row_group, col_group, row_blk, col_blk):
    rows, cols = full.shape
    keep = (jnp.arange(rows)[:, None] // row_group) == (jnp.arange(cols)[None, :] // col_group)
    full = jnp.where(keep, full, 0.0)
    return jnp.stack([full[i * row_blk:(i + 1) * row_blk, i * col_blk:(i + 1) * col_blk]
                      for i in range(rows // row_blk)]).astype(BF16)


def kernel(x_prompt, x_sample, cache_k, cache_v, page_table, state_conv, state_ssm_re, state_ssm_im, meta_tokens, norm_g, ffn1_w_gate, ffn1_w_up, ffn1_w_down, ffn2_w_gate, ffn2_w_up, ffn2_w_down, sb_w_qkv, sb_w_o, sb_q_norm, sb_k_norm, sb_logit_bias, conv_w_pw1, conv_w_dw, conv_norm_g, conv_w_pw2, ssm_A_re, ssm_A_im, ssm_log_dt, ssm_B_re, ssm_B_im, ssm_C_re, ssm_C_im, ssm_D, ssm_w_glu):
    nb, seq, d = x_prompt.shape
    nsb, ts_s, _ = x_sample.shape
    depth = norm_g.shape[0]
    n_meta = meta_tokens.shape[0]
    nh = sb_logit_bias.shape[1]
    hd = d // nh
    t_valid = seq + n_meta
    t_pad = _round_up(t_valid, SEQ_TILE)
    m_p = nb * t_pad
    m_s = nsb * ts_s
    tm = 512
    while m_p % tm or m_s % tm:
        tm //= 2
    assert tm >= SUBLANES and d % LANES == 0 and LANES % hd == 0

    meta = jnp.broadcast_to(meta_tokens[None], (nb, n_meta, d))
    pad = jnp.zeros((nb, t_pad - t_valid, d), F32)
    h = jnp.concatenate([jnp.concatenate([meta, x_prompt, pad], axis=1).reshape(m_p, d),
                         x_sample.reshape(m_s, d)], axis=0)

    page = cache_k.shape[2]
    cache_k = cache_k.reshape(cache_k.shape[:3] + (d,))
    cache_v = cache_v.reshape(cache_v.shape[:3] + (d,))
    idx = jnp.arange(SEQ_TILE)

    def tri_ones(n):
        i = jnp.arange(n)
        return jnp.concatenate([(i[:, None] > i[None, :]), jnp.ones((n, n), bool)],
                               axis=1).astype(BF16)

    head_ones = (idx[:, None] // hd == idx[None, :] // hd).astype(BF16)

    def rows_p(x):
        return x[:m_p].reshape(nb, t_pad, d)

    def rows_s(x):
        return x[m_p:].reshape(nsb, ts_s, d)

    new_k, new_v, new_conv, new_re, new_im = [], [], [], [], []
    for i in range(depth):
        kind, j = i % 3, i // 3
        h = _ffn(h, norm_g[i, 0][None], ffn1_w_gate[i].astype(BF16), ffn1_w_up[i].astype(BF16),
                 ffn1_w_down[i].astype(BF16), tm)
        g_mix = norm_g[i, 1][None]
        if kind == 0:
            q, k, v = _qkv(h, g_mix, sb_w_qkv[j].astype(BF16), jnp.tile(sb_q_norm[j], nh)[None],
                           jnp.tile(sb_k_norm[j], nh)[None], head_ones, hd, tm)
            bias = sb_logit_bias[j]
            o = _attn_prompt(q, k, v, bias, tri_ones(SEQ_TILE), nb, t_pad, hd)
            bias_rows = jnp.broadcast_to(jnp.repeat(bias, ts_s)[:, None], (nh * ts_s, page))
            o = _attn_sample(o, q, k, v, cache_k, cache_v, j, page_table, bias_rows,
                             tri_ones(page), m_p, nsb, ts_s, nh, hd)
            h = _proj_res(h, o, sb_w_o[j].astype(BF16), tm)
            new_k.append((rows_p(k)[:, :t_valid].reshape(nb, t_valid, nh, hd),
                          rows_s(k).reshape(nsb, ts_s, nh, hd)))
            new_v.append((rows_p(v)[:, :t_valid].reshape(nb, t_valid, nh, hd),
                          rows_s(v).reshape(nsb, ts_s, nh, hd)))
        elif kind == 1:
            u = _norm_glu(h, g_mix, conv_w_pw1[j].astype(BF16), tm)
            wdw, gn, w2 = conv_w_dw[j], conv_norm_g[j][None], conv_w_pw2[j].astype(BF16)
            width = wdw.shape[0]
            h = _conv_prompt(h, u, wdw, gn, w2, nb, t_pad)
            u_s = rows_s(u)
            hs_t = _conv_sample(rows_s(h).transpose(1, 0, 2), u_s.transpose(1, 0, 2),
                                state_conv[j].transpose(1, 0, 2), wdw, gn, w2)
            h = lax.dynamic_update_slice(h, hs_t.transpose(1, 0, 2).reshape(m_s, d), (m_p, 0))
            new_conv.append((rows_p(u)[:, t_valid - (width - 1):t_valid],
                             jnp.concatenate([state_conv[j], u_s], axis=1)[:, -(width - 1):]))
        else:
            ng, ns = ssm_A_re.shape[1:]
            gc = ssm_B_re.shape[3]
            gp = ng * ns
            pwre, pwim, bbre, bbim = _ssm_prep(
                ssm_A_re[j].reshape(1, gp), ssm_A_im[j].reshape(1, gp),
                jnp.repeat(ssm_log_dt[j], ns)[None],
                ssm_B_re[j].transpose(2, 0, 1).reshape(gc, gp),
                ssm_B_im[j].transpose(2, 0, 1).reshape(gc, gp))
            gpb = LANES // gc
            bb = [_block_diag_blocks(jnp.tile(x, (ng, 1)), gc, ns, LANES, gpb * ns)
                  for x in (bbre, bbim)]
            cc = [_block_diag_blocks(jnp.tile(x[j].transpose(0, 2, 1).reshape(gp, gc), (1, ng)),
                                     ns, gc, gpb * ns, LANES) for x in (ssm_C_re, ssm_C_im)]
            args = (norm_g[i, 1][None], bb[0], bb[1], cc[0], cc[1], pwre, pwim, ssm_D[j][None],
                    ssm_w_glu[j].astype(BF16))
            h, pre, pim = _ssm_prompt(h, *args, nb, t_pad, t_valid)
            h, sre, sim = _ssm_sample(h, *args, state_ssm_re[j].reshape(nsb, gp),
                                      state_ssm_im[j].reshape(nsb, gp), m_p, ts_s)
            new_re.append((pre.reshape(nb, ng, ns), sre.reshape(nsb, ng, ns)))
            new_im.append((pim.reshape(nb, ng, ns), sim.reshape(nsb, ng, ns)))
        h = _ffn(h, norm_g[i, 2][None], ffn2_w_gate[i].astype(BF16), ffn2_w_up[i].astype(BF16),
                 ffn2_w_down[i].astype(BF16), tm)

    def stack(pairs, which):
        return jnp.stack([p[which] for p in pairs])

    y_prompt = rows_p(h)[:, n_meta:t_valid]
    y_sample = rows_s(h)
    return (y_prompt, y_sample, stack(new_k, 0), stack(new_v, 0), stack(new_k, 1),
            stack(new_v, 1), stack(new_conv, 0), stack(new_conv, 1), stack(new_re, 0),
            stack(new_im, 0), stack(new_re, 1), stack(new_im, 1))
```

```python
import functools

import jax
import jax.numpy as jnp
from jax import lax
from jax.experimental import pallas as pl
from jax.experimental.pallas import tpu as pltpu

F32 = jnp.float32
BF16 = jnp.bfloat16
EPS = 1e-6
LANES = 128
SUBLANES = 8
VMEM_LIMIT = 56 * 1024 * 1024
SEQ_TILE = 128
SSM_LANE_CHUNK = 512


def _round_up(x, m):
    return (x + m - 1) // m * m


def _largest_tile(n, cap, mult=SUBLANES):
    best = mult
    for t in range(mult, cap + 1, mult):
        if n % t == 0:
            best = t
    return best


def _params(n_grid):
    return pltpu.CompilerParams(dimension_semantics=("arbitrary",) * n_grid,
                                vmem_limit_bytes=VMEM_LIMIT)


def _const_spec(shape):
    nd = len(shape)
    return pl.BlockSpec(shape, lambda *_: (0,) * nd, pipeline_mode=pl.Buffered(1))


def _layer_spec(w, layer):
    nd = w.ndim - 1
    return pl.BlockSpec((None,) + w.shape[1:], lambda *_: (layer,) + (0,) * nd,
                        pipeline_mode=pl.Buffered(1))


def _dot(a, b):
    return jnp.dot(a, b, preferred_element_type=F32)


def _dot_t(a, b):
    return lax.dot_general(a, b, (((1,), (1,)), ((), ())), preferred_element_type=F32)


def _rms(x, g):
    ms = jnp.mean(x * x, axis=-1, keepdims=True)
    return x * lax.rsqrt(ms + EPS) * g


def _sigmoid(x):
    return 1.0 / (1.0 + jnp.exp(-x))


def _ffn_kernel(h_ref, g_ref, wg_ref, wu_ref, wd_ref, o_ref, *, fc):
    h = h_ref[...]
    n = _rms(h, g_ref[...]).astype(BF16)
    acc = jnp.zeros(h.shape, F32)
    for c in range(wg_ref.shape[1] // fc):
        sl = slice(c * fc, (c + 1) * fc)
        gate = _dot(n, wg_ref[:, sl])
        up = _dot(n, wu_ref[:, sl])
        a = (gate * _sigmoid(gate) * up).astype(BF16)
        acc = acc + _dot(a, wd_ref[sl, :])
    o_ref[...] = h + 0.5 * acc


def _ffn(h, g, wg, wu, wd, layer, tm):
    m, d = h.shape
    f = wg.shape[2]
    fc = 256 if f % 256 == 0 else LANES
    row = pl.BlockSpec((tm, d), lambda i: (i, 0))
    return pl.pallas_call(
        functools.partial(_ffn_kernel, fc=fc),
        grid=(m // tm,),
        in_specs=[row, _const_spec((1, d)), _layer_spec(wg, layer), _layer_spec(wu, layer),
                  _layer_spec(wd, layer)],
        out_specs=row,
        out_shape=jax.ShapeDtypeStruct((m, d), F32),
        input_output_aliases={0: 0},
        compiler_params=_params(1),
        name="ffn",
    )(h, g, wg, wu, wd)


def _proj_res_kernel(h_ref, x_ref, w_ref, o_ref):
    o_ref[...] = h_ref[...] + _dot(x_ref[...].astype(BF16), w_ref[...])


def _proj_res(h, x, w, layer, tm):
    m, d = h.shape
    row = pl.BlockSpec((tm, d), lambda i: (i, 0))
    return pl.pallas_call(
        _proj_res_kernel,
        grid=(m // tm,),
        in_specs=[row, pl.BlockSpec((tm, x.shape[1]), lambda i: (i, 0)), _layer_spec(w, layer)],
        out_specs=row,
        out_shape=jax.ShapeDtypeStruct((m, d), F32),
        input_output_aliases={0: 0},
        compiler_params=_params(1),
        name="proj_res",
    )(h, x, w)


def _norm_glu_kernel(h_ref, g_ref, w_ref, o_ref):
    d = h_ref.shape[1]
    n = _rms(h_ref[...], g_ref[...]).astype(BF16)
    a = _dot(n, w_ref[:, :d])
    gate = _dot(n, w_ref[:, d:])
    o_ref[...] = a * _sigmoid(gate)


def _norm_glu(h, g, w, layer, tm):
    m, d = h.shape
    row = pl.BlockSpec((tm, d), lambda i: (i, 0))
    return pl.pallas_call(
        _norm_glu_kernel,
        grid=(m // tm,),
        in_specs=[row, _const_spec((1, d)), _layer_spec(w, layer)],
        out_specs=row,
        out_shape=jax.ShapeDtypeStruct((m, d), F32),
        compiler_params=_params(1),
        name="norm_glu",
    )(h, g, w)


def _qkv_kernel(h_ref, g_ref, w_ref, gq_ref, gk_ref, hm_ref, q_ref, k_ref, v_ref, *, hd):
    d = h_ref.shape[1]
    n = _rms(h_ref[...], g_ref[...]).astype(BF16)
    hm = hm_ref[...]

    def head_norm(x, gain):
        x2 = (x * x).astype(BF16)
        ms = jnp.concatenate(
            [_dot(x2[:, c * LANES:(c + 1) * LANES], hm) for c in range(d // LANES)], axis=1)
        return x * lax.rsqrt(ms * (1.0 / hd) + EPS) * gain

    q = head_norm(_dot(n, w_ref[:, :d]), gq_ref[...])
    q_ref[...] = q * (hd ** -0.5)
    k_ref[...] = head_norm(_dot(n, w_ref[:, d:2 * d]), gk_ref[...])
    v_ref[...] = _dot(n, w_ref[:, 2 * d:])


def _qkv(h, g, w, layer, gq, gk, hm, hd, tm):
    m, d = h.shape
    row = pl.BlockSpec((tm, d), lambda i: (i, 0))
    out = jax.ShapeDtypeStruct((m, d), F32)
    return pl.pallas_call(
        functools.partial(_qkv_kernel, hd=hd),
        grid=(m // tm,),
        in_specs=[row, _const_spec((1, d)), _layer_spec(w, layer), _const_spec((1, d)),
                  _const_spec((1, d)), _const_spec(hm.shape)],
        out_specs=[row, row, row],
        out_shape=[out, out, out],
        compiler_params=_params(1),
        name="qkv",
    )(h, g, w, gq, gk, hm)


def _softplus(z):
    return jnp.maximum(z, 0.0) + jnp.log(1.0 + jnp.exp(-jnp.abs(z)))


def _hi_lo(x):
    hi = x.astype(BF16)
    lo = (x - hi.astype(F32)).astype(BF16)
    return jnp.concatenate([hi, lo], axis=1)


def _suffix_matrix(n):
    i = jnp.arange(n)
    tri = i[:, None] >= i[None, :]
    return jnp.concatenate([tri, tri], axis=0).astype(BF16)


def _attn_prompt_kernel(bias_ref, q_ref, k_ref, v_ref, tri_ref, o_ref, k16, vm16, carry_ref,
                        acc_ref, *, hd, ts):
    hg = pl.program_id(1)
    qi = pl.program_id(2)
    hpl = LANES // hd
    nlb = q_ref.shape[1] // LANES
    lane = lax.broadcasted_iota(jnp.int32, (ts, LANES), 1)
    head_lanes = [lane // hd == j for j in range(hpl)]

    @pl.when(qi == 0)
    def _():
        k16[...] = k_ref[...].astype(BF16)
        v = v_ref[...]
        lanes = lax.broadcasted_iota(jnp.int32, v.shape, 1) % LANES
        for j in range(hpl):
            vm16[j] = jnp.where(lanes // hd == j, v, 0.0).astype(BF16)

    tri = tri_ref[...]
    qm = []
    for p in range(nlb):
        qp = q_ref[:, p * LANES:(p + 1) * LANES]
        qm.append(jnp.concatenate([jnp.where(m, qp, 0.0) for m in head_lanes], axis=0).astype(BF16))
    bias = [[bias_ref[(hg * nlb + p) * hpl + j] for j in range(hpl)] for p in range(nlb)]
    carry_ref[...] = jnp.zeros(carry_ref.shape, F32)
    acc_ref[...] = jnp.zeros(acc_ref.shape, F32)

    def tile(kt, mask):
        k0 = pl.multiple_of(kt * ts, ts)
        zs, sps = [], []
        for p in range(nlb):
            zp = _dot_t(qm[p], k16[pl.ds(k0, ts), p * LANES:(p + 1) * LANES])
            for j in range(hpl):
                z = zp[j * ts:(j + 1) * ts] + bias[p][j]
                sp = _softplus(z)
                if mask is not None:
                    sp = jnp.where(mask, sp, 0.0)
                zs.append(z)
                sps.append(sp)
        suffix = _dot(jnp.concatenate([_hi_lo(sp) for sp in sps], axis=0), tri)
        for p in range(nlb):
            ws = []
            for j in range(hpl):
                i = p * hpl + j
                carry = carry_ref[i]
                w = jnp.exp(zs[i] - suffix[i * ts:(i + 1) * ts] - carry)
                if mask is not None:
                    w = jnp.where(mask, w, 0.0)
                ws.append(w.astype(BF16))
                carry_ref[i] = carry + jnp.sum(sps[i], axis=1, keepdims=True)
            vm = jnp.concatenate(
                [vm16[j, pl.ds(k0, ts), p * LANES:(p + 1) * LANES] for j in range(hpl)], axis=0)
            acc_ref[p] += _dot(jnp.concatenate(ws, axis=1), vm)

    row = lax.broadcasted_iota(jnp.int32, (ts, ts), 0)
    col = lax.broadcasted_iota(jnp.int32, (ts, ts), 1)
    tile(qi, col < row)

    def body(i, c):
        tile(qi - 1 - i, None)
        return c

    lax.fori_loop(0, qi, body, 0)
    o_ref[...] = jnp.concatenate([acc_ref[p] for p in range(nlb)], axis=1)


def _attn_prompt(q, k, v, bias, tri, nb, t_pad, hd):
    m, d = q.shape
    ts = SEQ_TILE
    nqt = t_pad // ts
    hpl = LANES // hd
    wg = min(d, 4 * LANES)
    assert d % wg == 0
    seq = pl.BlockSpec((t_pad, wg), lambda b, hg, qi: (b, hg))
    tile = pl.BlockSpec((ts, wg), lambda b, hg, qi: (b * nqt + qi, hg))
    return pl.pallas_call(
        functools.partial(_attn_prompt_kernel, hd=hd, ts=ts),
        grid=(nb, d // wg, nqt),
        in_specs=[pl.BlockSpec(memory_space=pltpu.SMEM), tile, seq, seq, _const_spec(tri.shape)],
        out_specs=tile,
        out_shape=jax.ShapeDtypeStruct((m, d), F32),
        scratch_shapes=[pltpu.VMEM((t_pad, wg), BF16), pltpu.VMEM((hpl, t_pad, wg), BF16),
                        pltpu.VMEM((wg // hd, ts, ts), F32),
                        pltpu.VMEM((wg // LANES, ts, LANES), F32)],
        compiler_params=_params(3),
        name="attn_prompt",
    )(bias, q, k, v, tri)


def _attn_sample_kernel(pt_ref, q_ref, kn_ref, vn_ref, *refs, nh, hd, pps, n_steps):
    del pt_ref
    kc_refs, vc_refs = refs[:pps], refs[pps:2 * pps]
    bias_ref, tri_ref, _, o_ref, qbd_ref, carry_ref, acc_ref = refs[2 * pps:]
    s = pl.program_id(1)
    tq, d = q_ref.shape
    tk = kc_refs[0].shape[1]
    r = nh * tq
    hpl = LANES // hd
    rp = hpl * tq
    nlb = d // LANES
    kc = min(d, 2 * LANES)
    rk = kc // hd * tq
    tri = tri_ref[...]
    bias = bias_ref[...]

    def weights(z, mask, carry):
        sp = _softplus(z)
        if mask is not None:
            sp = jnp.where(mask, sp, 0.0)
        w = jnp.exp(z - _dot(_hi_lo(sp), tri) - carry)
        if mask is not None:
            w = jnp.where(mask, w, 0.0)
        return w.astype(BF16), carry + jnp.sum(sp, axis=1, keepdims=True)

    @pl.when(s == 0)
    def _():
        row_h = lax.broadcasted_iota(jnp.int32, (r, d), 0) // tq
        lane_h = lax.broadcasted_iota(jnp.int32, (r, d), 1) // hd
        qbd = jnp.where(row_h == lane_h, jnp.tile(q_ref[...], (nh, 1)), 0.0).astype(BF16)
        qbd_ref[...] = qbd
        pad = jnp.zeros((tk - tq, d), F32)
        kb = jnp.concatenate([kn_ref[...], pad], axis=0).astype(BF16)
        vb = jnp.concatenate([vn_ref[...], pad], axis=0).astype(BF16)
        qpos = lax.broadcasted_iota(jnp.int32, (r, tk), 0) % tq
        kpos = lax.broadcasted_iota(jnp.int32, (r, tk), 1)
        w, carry = weights(_dot_t(qbd, kb) + bias, kpos < qpos, jnp.zeros((r, tk), F32))
        carry_ref[...] = carry
        pv = _dot(w, vb)
        acc_ref[...] = jnp.concatenate(
            [pv[c * rp:(c + 1) * rp, c * LANES:(c + 1) * LANES] for c in range(nlb)], axis=0)

    zs, sps = [], []
    for i in range(pps):
        kt = kc_refs[i][...].astype(BF16)
        z = jnp.concatenate(
            [_dot(qbd_ref[c * rk:(c + 1) * rk, c * kc:(c + 1) * kc], kt[c * kc:(c + 1) * kc, :])
             for c in range(d // kc)], axis=0) + bias
        zs.append(z)
        sps.append(_softplus(z))
    suffix = _dot(jnp.concatenate([_hi_lo(sp) for sp in sps], axis=0), tri)
    carry = carry_ref[...]
    pv = None
    for i in range(pps):
        w = jnp.exp(zs[i] - suffix[i * r:(i + 1) * r] - carry).astype(BF16)
        carry = carry + jnp.sum(sps[i], axis=1, keepdims=True)
        vt = vc_refs[i][...].astype(BF16)
        pv_i = jnp.concatenate(
            [_dot_t(w[c * rp:(c + 1) * rp, :], vt[c * LANES:(c + 1) * LANES, :])
             for c in range(nlb)], axis=0)
        pv = pv_i if pv is None else pv + pv_i
    carry_ref[...] = carry
    acc_ref[...] += pv

    @pl.when(s == n_steps - 1)
    def _():
        lane_h = lax.broadcasted_iota(jnp.int32, (tq, LANES), 1) // hd
        cols = []
        for c in range(nlb):
            o = acc_ref[c * rp:c * rp + tq, :]
            for j in range(1, hpl):
                o = jnp.where(lane_h == j, acc_ref[c * rp + j * tq:c * rp + (j + 1) * tq, :], o)
            cols.append(o)
        o_ref[...] = jnp.concatenate(cols, axis=1)


def _attn_sample(o, q, k, v, cache_kt, cache_vt, layer, page_table, bias_rows, tri,
                 m_p, nb, tq, nh, hd):
    m, d = q.shape
    n_pages = page_table.shape[1]
    tk = cache_kt.shape[3]
    r = nh * tq
    pps = max(p for p in (1, 2, 3, 4) if n_pages % p == 0)
    n_steps = n_pages // pps
    new_rows = pl.BlockSpec((tq, d), lambda b, s, pt: (m_p // tq + b, 0))

    def page_spec(i):
        def page_map(b, s, pt):
            return (layer, pt[b * n_pages + n_pages - 1 - (s * pps + i)], 0, 0)
        return pl.BlockSpec((None, None, d, tk), page_map)

    pages = [page_spec(i) for i in range(pps)]
    grid_spec = pltpu.PrefetchScalarGridSpec(
        num_scalar_prefetch=1,
        grid=(nb, n_steps),
        in_specs=[new_rows, new_rows, new_rows] + pages + pages
        + [_const_spec(bias_rows.shape), _const_spec(tri.shape), pl.BlockSpec(memory_space=pl.ANY)],
        out_specs=new_rows,
        scratch_shapes=[pltpu.VMEM((r, d), BF16), pltpu.VMEM((r, tk), F32),
                        pltpu.VMEM((r, LANES), F32)],
    )
    return pl.pallas_call(
        functools.partial(_attn_sample_kernel, nh=nh, hd=hd, pps=pps, n_steps=n_steps),
        grid_spec=grid_spec,
        out_shape=jax.ShapeDtypeStruct((m, d), F32),
        input_output_aliases={6 + 2 * pps: 0},
        compiler_params=_params(2),
        name="attn_sample",
    )(page_table.reshape(-1), q, k, v, *([cache_kt] * pps), *([cache_vt] * pps), bias_rows, tri, o)


def _conv_prompt_kernel(u_ref, h_ref, wdw_ref, gn_ref, w2_ref, o_ref, ext_ref, *, ts, halo):
    ti = pl.program_id(1)
    width = wdw_ref.shape[0]

    @pl.when(ti == 0)
    def _():
        ext_ref[0:halo, :] = jnp.zeros((halo, ext_ref.shape[1]), F32)
        ext_ref[halo:, :] = u_ref[...]

    t0 = pl.multiple_of(ti * ts, ts)
    win = ext_ref[pl.ds(t0, ts + halo), :]
    off = halo - (width - 1)
    y = jnp.zeros((ts, win.shape[1]), F32)
    for j in range(width):
        y = y + win[off + j:off + j + ts, :] * wdw_ref[j:j + 1, :]
    yn = _rms(y, gn_ref[...])
    o_ref[...] = h_ref[...] + _dot((yn * _sigmoid(yn)).astype(BF16), w2_ref[...])


def _conv_prompt(h, u, wdw, gn, w2, layer, nb, t_pad):
    m, d = h.shape
    ts = SEQ_TILE
    nt = t_pad // ts
    halo = _round_up(wdw.shape[0] - 1, SUBLANES)
    tile = pl.BlockSpec((ts, d), lambda b, t: (b * nt + t, 0))
    return pl.pallas_call(
        functools.partial(_conv_prompt_kernel, ts=ts, halo=halo),
        grid=(nb, nt),
        in_specs=[pl.BlockSpec((t_pad, d), lambda b, t: (b, 0)), tile, _const_spec(wdw.shape),
                  _const_spec((1, d)), _layer_spec(w2, layer)],
        out_specs=tile,
        out_shape=jax.ShapeDtypeStruct((m, d), F32),
        input_output_aliases={1: 0},
        scratch_shapes=[pltpu.VMEM((halo + t_pad, d), F32)],
        compiler_params=_params(2),
        name="conv_prompt",
    )(u, h, wdw, gn, w2)


def _conv_sample_kernel(st_ref, u_ref, h_ref, wdw_ref, gn_ref, w2_ref, o_ref):
    width = wdw_ref.shape[0]
    ns = st_ref.shape[0]
    for t in range(u_ref.shape[0]):
        y = jnp.zeros(h_ref.shape[1:], F32)
        for j in range(width):
            i = t + j
            x = st_ref[i] if i < ns else u_ref[i - ns]
            y = y + x * wdw_ref[j:j + 1, :]
        yn = _rms(y, gn_ref[...])
        o_ref[t] = h_ref[t] + _dot((yn * _sigmoid(yn)).astype(BF16), w2_ref[...])


def _conv_sample(h_t, u_t, st_t, wdw, gn, w2, layer):
    t, nb, d = h_t.shape
    bs = _largest_tile(nb, 32)
    blk = lambda n: pl.BlockSpec((n, bs, d), lambda i: (0, i, 0))
    return pl.pallas_call(
        _conv_sample_kernel,
        grid=(nb // bs,),
        in_specs=[blk(st_t.shape[0]), blk(t), blk(t), _const_spec(wdw.shape),
                  _const_spec((1, d)), _layer_spec(w2, layer)],
        out_specs=blk(t),
        out_shape=jax.ShapeDtypeStruct((t, nb, d), F32),
        compiler_params=_params(1),
        name="conv_sample",
    )(st_t, u_t, h_t, wdw, gn, w2)


def _ssm_prep_kernel(are_ref, aim_ref, ldt_ref, bre_ref, bim_ref,
                     pwre_ref, pwim_ref, bbre_ref, bbim_ref):
    a_re, a_im = are_ref[...], aim_ref[...]
    dt = jnp.exp(ldt_ref[...])
    mag = jnp.exp(dt * a_re)
    ab_re, ab_im = mag * jnp.cos(dt * a_im), mag * jnp.sin(dt * a_im)
    den = a_re * a_re + a_im * a_im
    nr, ni = ab_re - 1.0, ab_im
    f_re = (nr * a_re + ni * a_im) / den
    f_im = (ni * a_re - nr * a_im) / den
    b_re, b_im = bre_ref[...], bim_ref[...]
    bbre_ref[...] = f_re * b_re - f_im * b_im
    bbim_ref[...] = f_re * b_im + f_im * b_re
    p_re, p_im = ab_re, ab_im
    for k in range(pwre_ref.shape[0]):
        pwre_ref[k:k + 1, :] = p_re
        pwim_ref[k:k + 1, :] = p_im
        p_re, p_im = p_re * ab_re - p_im * ab_im, p_re * ab_im + p_im * ab_re


def _ssm_prep(a_re, a_im, ldt, b_re, b_im):
    gp = a_re.shape[1]
    gc = b_re.shape[0]
    vec = jax.ShapeDtypeStruct((SUBLANES, gp), F32)
    mat = jax.ShapeDtypeStruct((gc, gp), F32)
    return pl.pallas_call(
        _ssm_prep_kernel,
        out_shape=[vec, vec, mat, mat],
        compiler_params=pltpu.CompilerParams(vmem_limit_bytes=VMEM_LIMIT),
        name="ssm_prep",
    )(a_re, a_im, ldt, b_re, b_im)


def _ssm_kernel(*refs, chained, t_last, lc):
    if chained:
        (h_ref, g_ref, bbre_ref, bbim_ref, ccre_ref, ccim_ref, pwre_ref, pwim_ref, dsk_ref,
         wglu_ref, o_ref, stre_ref, stim_ref, sre, sim, cre, cim) = refs
    else:
        (h_ref, g_ref, bbre_ref, bbim_ref, ccre_ref, ccim_ref, pwre_ref, pwim_ref, dsk_ref,
         wglu_ref, h0re_ref, h0im_ref, o_ref, stre_ref, stim_ref, sre, sim) = refs
    rows, d = h_ref.shape
    gp = sre.shape[1]
    nblk = d // LANES
    cw = gp // nblk

    h = h_ref[...]
    n = _rms(h, g_ref[...])
    nb16 = n.astype(BF16)
    for c in range(nblk):
        x = nb16[:, c * LANES:(c + 1) * LANES]
        sre[:, c * cw:(c + 1) * cw] = _dot(x, bbre_ref[c])
        sim[:, c * cw:(c + 1) * cw] = _dot(x, bbim_ref[c])

    if chained:
        ti = pl.program_id(1)

        @pl.when(ti == 0)
        def _():
            cre[...] = jnp.zeros(cre.shape, F32)
            cim[...] = jnp.zeros(cim.shape, F32)

    rowi = lax.broadcasted_iota(jnp.int32, (SUBLANES, lc), 0)
    for c in range(gp // lc):
        ls = slice(c * lc, (c + 1) * lc)
        p_re, p_im = pwre_ref[:, ls], pwim_ref[:, ls]

        def body(r, carry, ls=ls, p_re=p_re, p_im=p_im):
            r0 = pl.multiple_of(r * SUBLANES, SUBLANES)
            if chained:
                c_re, c_im = carry
            else:
                c_re, c_im = h0re_ref[pl.ds(r, 1), ls], h0im_ref[pl.ds(r, 1), ls]
            x_re, x_im = sre[pl.ds(r0, SUBLANES), ls], sim[pl.ds(r0, SUBLANES), ls]
            for dist in (1, 2, 4):
                a_re, a_im = p_re[dist - 1:dist], p_im[dist - 1:dist]
                s_re = jnp.where(rowi >= dist, pltpu.roll(x_re, dist, axis=0), 0.0)
                s_im = jnp.where(rowi >= dist, pltpu.roll(x_im, dist, axis=0), 0.0)
                x_re, x_im = (x_re + a_re * s_re - a_im * s_im,
                              x_im + a_re * s_im + a_im * s_re)
            x_re, x_im = (x_re + p_re * c_re - p_im * c_im,
                          x_im + p_re * c_im + p_im * c_re)
            sre[pl.ds(r0, SUBLANES), ls] = x_re
            sim[pl.ds(r0, SUBLANES), ls] = x_im
            if chained:
                return x_re[SUBLANES - 1:], x_im[SUBLANES - 1:]
            stre_ref[pl.ds(r, 1), ls] = x_re[SUBLANES - 1:]
            stim_ref[pl.ds(r, 1), ls] = x_im[SUBLANES - 1:]
            return carry

        if chained:
            c_re, c_im = lax.fori_loop(0, rows // SUBLANES, body, (cre[:, ls], cim[:, ls]))
            cre[:, ls] = c_re
            cim[:, ls] = c_im
        else:
            lax.fori_loop(0, rows // SUBLANES, body, 0)

    if chained:
        @pl.when(ti == t_last // rows)
        def _():
            stre_ref[0] = sre[t_last % rows:t_last % rows + 1, :]
            stim_ref[0] = sim[t_last % rows:t_last % rows + 1, :]

    hre = sre[...].astype(BF16)
    him = sim[...].astype(BF16)
    y = jnp.concatenate(
        [_dot(hre[:, c * cw:(c + 1) * cw], ccre_ref[c]) - _dot(him[:, c * cw:(c + 1) * cw], ccim_ref[c])
         for c in range(nblk)], axis=1)
    y = (y + dsk_ref[...] * n).astype(BF16)
    za = _dot(y, wglu_ref[:, :d])
    zb = _dot(y, wglu_ref[:, d:])
    o_ref[...] = h + za * _sigmoid(zb)


def _ssm_common_specs(d, bb, cc, pw, wglu, layer):
    return [_const_spec((1, d)), _const_spec(bb.shape), _const_spec(bb.shape),
            _const_spec(cc.shape), _const_spec(cc.shape), _const_spec(pw.shape),
            _const_spec(pw.shape), _const_spec((1, d)), _layer_spec(wglu, layer)]


def _ssm_prompt(h, g, bbre, bbim, ccre, ccim, pwre, pwim, dsk, wglu, layer, nb, t_pad, t_valid):
    m, d = h.shape
    gp = pwre.shape[1]
    tt = _largest_tile(t_pad, 256)
    nt = t_pad // tt
    lc = min(SSM_LANE_CHUNK, gp)
    tile = pl.BlockSpec((tt, d), lambda b, t: (b * nt + t, 0))
    st = pl.BlockSpec((1, 1, gp), lambda b, t: (b, 0, 0))
    st_shape = jax.ShapeDtypeStruct((nb, 1, gp), F32)
    return pl.pallas_call(
        functools.partial(_ssm_kernel, chained=True, t_last=t_valid - 1, lc=lc),
        grid=(nb, nt),
        in_specs=[tile] + _ssm_common_specs(d, bbre, ccre, pwre, wglu, layer),
        out_specs=[tile, st, st],
        out_shape=[jax.ShapeDtypeStruct((m, d), F32), st_shape, st_shape],
        input_output_aliases={0: 0},
        scratch_shapes=[pltpu.VMEM((tt, gp), F32), pltpu.VMEM((tt, gp), F32),
                        pltpu.VMEM((1, gp), F32), pltpu.VMEM((1, gp), F32)],
        compiler_params=_params(2),
        name="ssm_prompt",
    )(h, g, bbre, bbim, ccre, ccim, pwre, pwim, dsk, wglu)


def _ssm_sample(h, g, bbre, bbim, ccre, ccim, pwre, pwim, dsk, wglu, layer, h0re, h0im, m_p, t):
    m, d = h.shape
    assert t == SUBLANES, "the sample scan handles one 8-token block per sequence"
    nb, gp = h0re.shape
    bs = _largest_tile(nb, 32)
    rows = bs * t
    lc = min(SSM_LANE_CHUNK, gp)
    tile = pl.BlockSpec((rows, d), lambda i: (m_p // rows + i, 0))
    st = pl.BlockSpec((bs, gp), lambda i: (i, 0))
    st_shape = jax.ShapeDtypeStruct((nb, gp), F32)
    return pl.pallas_call(
        functools.partial(_ssm_kernel, chained=False, t_last=0, lc=lc),
        grid=(nb // bs,),
        in_specs=[tile] + _ssm_common_specs(d, bbre, ccre, pwre, wglu, layer) + [st, st],
        out_specs=[tile, st, st],
        out_shape=[jax.ShapeDtypeStruct((m, d), F32), st_shape, st_shape],
        input_output_aliases={0: 0},
        scratch_shapes=[pltpu.VMEM((rows, gp), F32), pltpu.VMEM((rows, gp), F32)],
        compiler_params=_params(1),
        name="ssm_sample",
    )(h, g, bbre, bbim, ccre, ccim, pwre, pwim, dsk, wglu, h0re, h0im)


def _block_diag_blocks(full, row_group, col_group, row_blk, col_blk):
    rows, cols = full.shape
    keep = (jnp.arange(rows)[:, None] // row_group) == (jnp.arange(cols)[None, :] // col_group)
    full = jnp.where(keep, full, 0.0)
    return jnp.stack([full[i * row_blk:(i + 1) * row_blk, i * col_blk:(i + 1) * col_blk]
                      for i in range(rows // row_blk)]).astype(BF16)


def kernel(x_prompt, x_sample, cache_k, cache_v, page_table, state_conv, state_ssm_re, state_ssm_im, meta_tokens, norm_g, ffn1_w_gate, ffn1_w_up, ffn1_w_down, ffn2_w_gate, ffn2_w_up, ffn2_w_down, sb_w_qkv, sb_w_o, sb_q_norm, sb_k_norm, sb_logit_bias, conv_w_pw1, conv_w_dw, conv_norm_g, conv_w_pw2, ssm_A_re, ssm_A_im, ssm_log_dt, ssm_B_re, ssm_B_im, ssm_C_re, ssm_C_im, ssm_D, ssm_w_glu):
    nb, seq, d = x_prompt.shape
    nsb, ts_s, _ = x_sample.shape
    depth = norm_g.shape[0]
    n_meta = meta_tokens.shape[0]
    nh = sb_logit_bias.shape[1]
    hd = d // nh
    t_valid = seq + n_meta
    t_pad = _round_up(t_valid, SEQ_TILE)
    m_p = nb * t_pad
    m_s = nsb * ts_s
    tm = 512
    while m_p % tm or m_s % tm:
        tm //= 2
    assert tm >= SUBLANES and d % LANES == 0 and LANES % hd == 0

    meta = jnp.broadcast_to(meta_tokens[None], (nb, n_meta, d))
    pad = jnp.zeros((nb, t_pad - t_valid, d), F32)
    h = jnp.concatenate([jnp.concatenate([meta, x_prompt, pad], axis=1).reshape(m_p, d),
                         x_sample.reshape(m_s, d)], axis=0)

    page = cache_k.shape[2]
    cache_kt = cache_k.transpose(0, 1, 3, 4, 2).reshape(cache_k.shape[:2] + (d, page))
    cache_vt = cache_v.transpose(0, 1, 3, 4, 2).reshape(cache_v.shape[:2] + (d, page))
    idx = jnp.arange(LANES)
    head_ones = (idx[:, None] // hd == idx[None, :] // hd).astype(BF16)

    ffn1 = [w.astype(BF16) for w in (ffn1_w_gate, ffn1_w_up, ffn1_w_down)]
    ffn2 = [w.astype(BF16) for w in (ffn2_w_gate, ffn2_w_up, ffn2_w_down)]
    w_qkv, w_o = sb_w_qkv.astype(BF16), sb_w_o.astype(BF16)
    w_pw1, w_pw2 = conv_w_pw1.astype(BF16), conv_w_pw2.astype(BF16)
    w_glu = ssm_w_glu.astype(BF16)

    def rows_p(x):
        return x[:m_p].reshape(nb, t_pad, d)

    def rows_s(x):
        return x[m_p:].reshape(nsb, ts_s, d)

    new_k, new_v, new_conv, new_re, new_im = [], [], [], [], []
    for i in range(depth):
        kind, j = i % 3, i // 3
        h = _ffn(h, norm_g[i, 0][None], *ffn1, i, tm)
        g_mix = norm_g[i, 1][None]
        if kind == 0:
            q, k, v = _qkv(h, g_mix, w_qkv, j, jnp.tile(sb_q_norm[j], nh)[None],
                           jnp.tile(sb_k_norm[j], nh)[None], head_ones, hd, tm)
            bias = sb_logit_bias[j]
            o = _attn_prompt(q, k, v, bias, _suffix_matrix(SEQ_TILE), nb, t_pad, hd)
            bias_rows = jnp.broadcast_to(jnp.repeat(bias, ts_s)[:, None], (nh * ts_s, page))
            o = _attn_sample(o, q, k, v, cache_kt, cache_vt, j, page_table, bias_rows,
                             _suffix_matrix(page), m_p, nsb, ts_s, nh, hd)
            h = _proj_res(h, o, w_o, j, tm)
            new_k.append((rows_p(k)[:, :t_valid].reshape(nb, t_valid, nh, hd),
                          rows_s(k).reshape(nsb, ts_s, nh, hd)))
            new_v.append((rows_p(v)[:, :t_valid].reshape(nb, t_valid, nh, hd),
                          rows_s(v).reshape(nsb, ts_s, nh, hd)))
        elif kind == 1:
            u = _norm_glu(h, g_mix, w_pw1, j, tm)
            wdw, gn = conv_w_dw[j], conv_norm_g[j][None]
            width = wdw.shape[0]
            h = _conv_prompt(h, u, wdw, gn, w_pw2, j, nb, t_pad)
            u_s = rows_s(u)
            hs_t = _conv_sample(rows_s(h).transpose(1, 0, 2), u_s.transpose(1, 0, 2),
                                state_conv[j].transpose(1, 0, 2), wdw, gn, w_pw2, j)
            h = lax.dynamic_update_slice(h, hs_t.transpose(1, 0, 2).reshape(m_s, d), (m_p, 0))
            new_conv.append((rows_p(u)[:, t_valid - (width - 1):t_valid],
                             jnp.concatenate([state_conv[j], u_s], axis=1)[:, -(width - 1):]))
        else:
            ng, ns = ssm_A_re.shape[1:]
            gc = ssm_B_re.shape[3]
            gp = ng * ns
            pwre, pwim, bbre, bbim = _ssm_prep(
                ssm_A_re[j].reshape(1, gp), ssm_A_im[j].reshape(1, gp),
                jnp.repeat(ssm_log_dt[j], ns)[None],
                ssm_B_re[j].transpose(2, 0, 1).reshape(gc, gp),
                ssm_B_im[j].transpose(2, 0, 1).reshape(gc, gp))
            gpb = LANES // gc
            bb = [_block_diag_blocks(jnp.tile(x, (ng, 1)), gc, ns, LANES, gpb * ns)
                  for x in (bbre, bbim)]
            cc = [_block_diag_blocks(jnp.tile(x[j].transpose(0, 2, 1).reshape(gp, gc), (1, ng)),
                                     ns, gc, gpb * ns, LANES) for x in (ssm_C_re, ssm_C_im)]
            args = (g_mix, bb[0], bb[1], cc[0], cc[1], pwre, pwim, ssm_D[j][None], w_glu, j)
            h, pre, pim = _ssm_prompt(h, *args, nb, t_pad, t_valid)
            h, sre, sim = _ssm_sample(h, *args, state_ssm_re[j].reshape(nsb, gp),
                                      state_ssm_im[j].reshape(nsb, gp), m_p, ts_s)
            new_re.append((pre.reshape(nb, ng, ns), sre.reshape(nsb, ng, ns)))
            new_im.append((pim.reshape(nb, ng, ns), sim.reshape(nsb, ng, ns)))
        h = _ffn(h, norm_g[i, 2][None], *ffn2, i, tm)

    def stack(pairs, which):
        return jnp.stack([p[which] for p in pairs])

    y_prompt = rows_p(h)[:, n_meta:t_valid]
    y_sample = rows_s(h)
    return (y_prompt, y_sample, stack(new_k, 0), stack(new_v, 0), stack(new_k, 1),
            stack(new_v, 1), stack(new_conv, 0), stack(new_conv, 1), stack(new_re, 0),
            stack(new_im, 0), stack(new_re, 1), stack(new_im, 1))
```

```python
import functools

import jax
import jax.numpy as jnp
from jax import lax
from jax.experimental import pallas as pl
from jax.experimental.pallas import tpu as pltpu

F32 = jnp.float32
BF16 = jnp.bfloat16
EPS = 1e-6
LANES = 128
SUBLANES = 8
VMEM_LIMIT = 56 * 1024 * 1024
SEQ_TILE = 128
Q_TILE = 256
ROW_TILE = 512
SSM_LANE_CHUNK = 512


def _round_up(x, m):
    return (x + m - 1) // m * m


def _largest_tile(n, cap, mult=SUBLANES):
    best = mult
    for t in range(mult, cap + 1, mult):
        if n % t == 0:
            best = t
    return best


def _row_tile(m):
    tm = ROW_TILE
    while m % tm:
        tm //= 2
    assert tm >= SUBLANES
    return tm


def _params(n_grid):
    return pltpu.CompilerParams(dimension_semantics=("arbitrary",) * n_grid,
                                vmem_limit_bytes=VMEM_LIMIT)


def _const_spec(shape):
    nd = len(shape)
    return pl.BlockSpec(shape, lambda *_: (0,) * nd, pipeline_mode=pl.Buffered(1))


def _layer_spec(w, layer):
    nd = w.ndim - 1
    return pl.BlockSpec((None,) + w.shape[1:], lambda *_: (layer,) + (0,) * nd,
                        pipeline_mode=pl.Buffered(1))


def _dot(a, b):
    return jnp.dot(a, b, preferred_element_type=F32)


def _dot_t(a, b):
    return lax.dot_general(a, b, (((1,), (1,)), ((), ())), preferred_element_type=F32)


def _rms(x, g):
    ms = jnp.mean(x * x, axis=-1, keepdims=True)
    return x * lax.rsqrt(ms + EPS) * g


def _sigmoid(x):
    return 1.0 / (1.0 + jnp.exp(-x))


def _ffn_kernel(h_ref, g_ref, wg_ref, wu_ref, wd_ref, o_ref, *, fc):
    h = h_ref[...]
    n = _rms(h, g_ref[...]).astype(BF16)
    acc = jnp.zeros(h.shape, F32)
    for c in range(wg_ref.shape[1] // fc):
        sl = slice(c * fc, (c + 1) * fc)
        gate = _dot(n, wg_ref[:, sl])
        up = _dot(n, wu_ref[:, sl])
        a = (gate * _sigmoid(gate) * up).astype(BF16)
        acc = acc + _dot(a, wd_ref[sl, :])
    o_ref[...] = h + 0.5 * acc


def _ffn(h, g, wg, wu, wd, layer):
    m, d = h.shape
    tm = _row_tile(m)
    f = wg.shape[2]
    fc = 256 if f % 256 == 0 else LANES
    row = pl.BlockSpec((tm, d), lambda i: (i, 0))
    return pl.pallas_call(
        functools.partial(_ffn_kernel, fc=fc),
        grid=(m // tm,),
        in_specs=[row, _const_spec((1, d)), _layer_spec(wg, layer), _layer_spec(wu, layer),
                  _layer_spec(wd, layer)],
        out_specs=row,
        out_shape=jax.ShapeDtypeStruct((m, d), F32),
        input_output_aliases={0: 0},
        compiler_params=_params(1),
        name="ffn",
    )(h, g, wg, wu, wd)


def _proj_res_kernel(h_ref, x_ref, w_ref, o_ref):
    o_ref[...] = h_ref[...] + _dot(x_ref[...].astype(BF16), w_ref[...])


def _proj_res(h, x, w, layer):
    m, d = h.shape
    tm = _row_tile(m)
    row = pl.BlockSpec((tm, d), lambda i: (i, 0))
    return pl.pallas_call(
        _proj_res_kernel,
        grid=(m // tm,),
        in_specs=[row, pl.BlockSpec((tm, x.shape[1]), lambda i: (i, 0)), _layer_spec(w, layer)],
        out_specs=row,
        out_shape=jax.ShapeDtypeStruct((m, d), F32),
        input_output_aliases={0: 0},
        compiler_params=_params(1),
        name="proj_res",
    )(h, x, w)


def _norm_glu_kernel(h_ref, g_ref, w_ref, o_ref):
    d = h_ref.shape[1]
    n = _rms(h_ref[...], g_ref[...]).astype(BF16)
    a = _dot(n, w_ref[:, :d])
    gate = _dot(n, w_ref[:, d:])
    o_ref[...] = a * _sigmoid(gate)


def _norm_glu(h, g, w, layer):
    m, d = h.shape
    tm = _row_tile(m)
    row = pl.BlockSpec((tm, d), lambda i: (i, 0))
    return pl.pallas_call(
        _norm_glu_kernel,
        grid=(m // tm,),
        in_specs=[row, _const_spec((1, d)), _layer_spec(w, layer)],
        out_specs=row,
        out_shape=jax.ShapeDtypeStruct((m, d), F32),
        compiler_params=_params(1),
        name="norm_glu",
    )(h, g, w)


def _qkv_kernel(h_ref, g_ref, w_ref, gq_ref, gk_ref, hm_ref, q_ref, k_ref, v_ref, *, hd):
    d = h_ref.shape[1]
    n = _rms(h_ref[...], g_ref[...]).astype(BF16)
    hm = hm_ref[...]

    def head_norm(x, gain):
        x2 = (x * x).astype(BF16)
        ms = jnp.concatenate(
            [_dot(x2[:, c * LANES:(c + 1) * LANES], hm) for c in range(d // LANES)], axis=1)
        return x * lax.rsqrt(ms * (1.0 / hd) + EPS) * gain

    q = head_norm(_dot(n, w_ref[:, :d]), gq_ref[...])
    q_ref[...] = q * (hd ** -0.5)
    k_ref[...] = head_norm(_dot(n, w_ref[:, d:2 * d]), gk_ref[...])
    v_ref[...] = _dot(n, w_ref[:, 2 * d:])


def _qkv(h, g, w, layer, gq, gk, hm, hd):
    m, d = h.shape
    tm = _row_tile(m)
    row = pl.BlockSpec((tm, d), lambda i: (i, 0))
    out = jax.ShapeDtypeStruct((m, d), F32)
    return pl.pallas_call(
        functools.partial(_qkv_kernel, hd=hd),
        grid=(m // tm,),
        in_specs=[row, _const_spec((1, d)), _layer_spec(w, layer), _const_spec((1, d)),
                  _const_spec((1, d)), _const_spec(hm.shape)],
        out_specs=[row, row, row],
        out_shape=[out, out, out],
        compiler_params=_params(1),
        name="qkv",
    )(h, g, w, gq, gk, hm)


def _softplus(z):
    return jnp.maximum(z, 0.0) + jnp.log(1.0 + jnp.exp(-jnp.abs(z)))


def _hi_lo(x):
    hi = x.astype(BF16)
    lo = (x - hi.astype(F32)).astype(BF16)
    return jnp.concatenate([hi, lo], axis=1)


def _suffix_matrix(n, parts):
    i = jnp.arange(n)
    tri = i[:, None] >= i[None, :]
    return jnp.concatenate([tri] * parts, axis=0).astype(BF16)


def _attn_prompt_kernel(bias_ref, q_ref, k_ref, v_ref, tri_ref, o_ref, k16, vm16, carry_ref,
                        acc_ref, *, hd, tk):
    hg = pl.program_id(1)
    qi = pl.program_id(2)
    tq = q_ref.shape[0]
    t_pad = k_ref.shape[0]
    nsub = tq // tk
    hpl = LANES // hd
    nlb = q_ref.shape[1] // LANES
    nh = nlb * hpl
    lane = lax.broadcasted_iota(jnp.int32, (tq, LANES), 1)

    @pl.when(qi == 0)
    def _():
        k16[0:t_pad, :] = k_ref[...].astype(BF16)
        v = v_ref[...]
        lanes = lax.broadcasted_iota(jnp.int32, v.shape, 1) % LANES
        tail = k16.shape[0] - t_pad
        for j in range(hpl):
            vm16[j, 0:t_pad, :] = jnp.where(lanes // hd == j, v, 0.0).astype(BF16)
            if tail:
                vm16[j, t_pad:, :] = jnp.zeros((tail, k16.shape[1]), BF16)
        if tail:
            k16[t_pad:, :] = jnp.zeros((tail, k16.shape[1]), BF16)

    tri = tri_ref[...]
    qm = []
    for p in range(nlb):
        qp = q_ref[:, p * LANES:(p + 1) * LANES]
        qm.append(jnp.concatenate([jnp.where(lane // hd == j, qp, 0.0) for j in range(hpl)],
                                  axis=0).astype(BF16))
    bias = [[bias_ref[(hg * nlb + p) * hpl + j] for j in range(hpl)] for p in range(nlb)]
    carry_ref[...] = jnp.zeros(carry_ref.shape, F32)
    acc_ref[...] = jnp.zeros(acc_ref.shape, F32)
    qpos = qi * tq + lax.broadcasted_iota(jnp.int32, (tq, tk), 0)
    kcol = lax.broadcasted_iota(jnp.int32, (tq, tk), 1)

    def tile(kt, masked):
        k0 = pl.multiple_of(kt * tk, tk)
        mask = (k0 + kcol < qpos) if masked else None
        zs, sps = [], []
        for p in range(nlb):
            zp = _dot_t(qm[p], k16[pl.ds(k0, tk), p * LANES:(p + 1) * LANES])
            for j in range(hpl):
                z = zp[j * tq:(j + 1) * tq] + bias[p][j]
                sp = _softplus(z)
                if masked:
                    sp = jnp.where(mask, sp, 0.0)
                zs.append(z)
                sps.append(sp)
        suffix = _dot(jnp.concatenate([sp.astype(BF16) for sp in sps], axis=0), tri)
        for p in range(nlb):
            ws = []
            for j in range(hpl):
                i = p * hpl + j
                carry = carry_ref[i]
                w = jnp.exp(zs[i] - suffix[i * tq:(i + 1) * tq] - carry)
                if masked:
                    w = jnp.where(mask, w, 0.0)
                ws.append(w.astype(BF16))
                carry_ref[i] = carry + jnp.sum(sps[i], axis=1, keepdims=True)
            vm = jnp.concatenate(
                [vm16[j, pl.ds(k0, tk), p * LANES:(p + 1) * LANES] for j in range(hpl)], axis=0)
            acc_ref[p] += _dot(jnp.concatenate(ws, axis=1), vm)

    for s in reversed(range(nsub)):
        tile(qi * nsub + s, True)

    def body(i, c):
        tile(qi * nsub - 1 - i, False)
        return c

    lax.fori_loop(0, qi * nsub, body, 0)
    o_ref[...] = jnp.concatenate([acc_ref[p] for p in range(nlb)], axis=1)


def _attn_prompt(q, k, v, bias, tri, hd):
    nb, t_pad, d = q.shape
    tk = SEQ_TILE
    tq = Q_TILE
    nqt = pl.cdiv(t_pad, tq)
    hpl = LANES // hd
    wg = min(d, 4 * LANES)
    assert d % wg == 0 and tq % tk == 0 and t_pad % tk == 0
    seq = pl.BlockSpec((None, t_pad, wg), lambda b, hg, qi: (b, 0, hg))
    tile = pl.BlockSpec((None, tq, wg), lambda b, hg, qi: (b, qi, hg))
    return pl.pallas_call(
        functools.partial(_attn_prompt_kernel, hd=hd, tk=tk),
        grid=(nb, d // wg, nqt),
        in_specs=[pl.BlockSpec(memory_space=pltpu.SMEM), tile, seq, seq, _const_spec(tri.shape)],
        out_specs=tile,
        out_shape=jax.ShapeDtypeStruct((nb, t_pad, d), F32),
        scratch_shapes=[pltpu.VMEM((nqt * tq, wg), BF16), pltpu.VMEM((hpl, nqt * tq, wg), BF16),
                        pltpu.VMEM((wg // hd, tq, tk), F32),
                        pltpu.VMEM((wg // LANES, tq, LANES), F32)],
        compiler_params=_params(3),
        name="attn_prompt",
    )(bias, q, k, v, tri)


def _attn_sample_kernel(pt_ref, q_ref, kn_ref, vn_ref, *refs, nh, hd, pps, n_steps):
    del pt_ref
    kc_refs, vc_refs = refs[:pps], refs[pps:2 * pps]
    bias_ref, tri_ref, o_ref, qbd_ref, carry_ref, acc_ref = refs[2 * pps:]
    s = pl.program_id(1)
    tq, d = q_ref.shape
    tk = kc_refs[0].shape[1]
    r = nh * tq
    hpl = LANES // hd
    rp = hpl * tq
    nlb = d // LANES
    kc = min(d, 2 * LANES)
    rk = kc // hd * tq
    tri = tri_ref[...]
    bias = bias_ref[...]

    @pl.when(s == 0)
    def _():
        row_h = lax.broadcasted_iota(jnp.int32, (r, d), 0) // tq
        lane_h = lax.broadcasted_iota(jnp.int32, (r, d), 1) // hd
        qbd = jnp.where(row_h == lane_h, jnp.tile(q_ref[...], (nh, 1)), 0.0).astype(BF16)
        qbd_ref[...] = qbd
        pad = jnp.zeros((tk - tq, d), F32)
        kb = jnp.concatenate([kn_ref[...], pad], axis=0).astype(BF16)
        vb = jnp.concatenate([vn_ref[...], pad], axis=0).astype(BF16)
        qpos = lax.broadcasted_iota(jnp.int32, (r, tk), 0) % tq
        kpos = lax.broadcasted_iota(jnp.int32, (r, tk), 1)
        mask = kpos < qpos
        z = _dot_t(qbd, kb) + bias
        sp = jnp.where(mask, _softplus(z), 0.0)
        w = jnp.where(mask, jnp.exp(z - _dot(_hi_lo(sp), tri)), 0.0).astype(BF16)
        carry_ref[...] = jnp.zeros((r, tk), F32) + jnp.sum(sp, axis=1, keepdims=True)
        pv = _dot(w, vb)
        acc_ref[...] = jnp.concatenate(
            [pv[c * rp:(c + 1) * rp, c * LANES:(c + 1) * LANES] for c in range(nlb)], axis=0)

    kcat = jnp.concatenate([kc_refs[i][...].astype(BF16) for i in range(pps)], axis=1)
    zcat = jnp.concatenate(
        [_dot(qbd_ref[c * rk:(c + 1) * rk, c * kc:(c + 1) * kc], kcat[c * kc:(c + 1) * kc, :])
         for c in range(d // kc)], axis=0)
    zs = [zcat[:, i * tk:(i + 1) * tk] + bias for i in range(pps)]
    sps = [_softplus(z) for z in zs]
    suffix = _dot(jnp.concatenate([_hi_lo(sp) for sp in sps], axis=0), tri)
    carry = carry_ref[...]
    ws = []
    for i in range(pps):
        ws.append(jnp.exp(zs[i] - suffix[i * r:(i + 1) * r] - carry).astype(BF16))
        carry = carry + jnp.sum(sps[i], axis=1, keepdims=True)
    carry_ref[...] = carry
    wcat = jnp.concatenate(ws, axis=1)
    vcat = jnp.concatenate([vc_refs[i][...].astype(BF16) for i in range(pps)], axis=1)
    acc_ref[...] += jnp.concatenate(
        [_dot_t(wcat[c * rp:(c + 1) * rp, :], vcat[c * LANES:(c + 1) * LANES, :])
         for c in range(nlb)], axis=0)

    @pl.when(s == n_steps - 1)
    def _():
        lane_h = lax.broadcasted_iota(jnp.int32, (tq, LANES), 1) // hd
        cols = []
        for c in range(nlb):
            o = acc_ref[c * rp:c * rp + tq, :]
            for j in range(1, hpl):
                o = jnp.where(lane_h == j, acc_ref[c * rp + j * tq:c * rp + (j + 1) * tq, :], o)
            cols.append(o)
        o_ref[...] = jnp.concatenate(cols, axis=1)


def _attn_sample(q, k, v, cache_kt, cache_vt, layer, page_table, bias_rows, tri, tq, nh, hd):
    m, d = q.shape
    nb, n_pages = page_table.shape
    tk = cache_kt.shape[3]
    r = nh * tq
    pps = max(p for p in (1, 2, 3, 4) if n_pages % p == 0)
    n_steps = n_pages // pps
    new_rows = pl.BlockSpec((tq, d), lambda b, s, pt: (b, 0))

    def page_spec(i):
        def page_map(b, s, pt):
            return (layer, pt[b * n_pages + n_pages - 1 - (s * pps + i)], 0, 0)
        return pl.BlockSpec((None, None, d, tk), page_map)

    pages = [page_spec(i) for i in range(pps)]
    grid_spec = pltpu.PrefetchScalarGridSpec(
        num_scalar_prefetch=1,
        grid=(nb, n_steps),
        in_specs=[new_rows, new_rows, new_rows] + pages + pages
        + [_const_spec(bias_rows.shape), _const_spec(tri.shape)],
        out_specs=new_rows,
        scratch_shapes=[pltpu.VMEM((r, d), BF16), pltpu.VMEM((r, tk), F32),
                        pltpu.VMEM((r, LANES), F32)],
    )
    return pl.pallas_call(
        functools.partial(_attn_sample_kernel, nh=nh, hd=hd, pps=pps, n_steps=n_steps),
        grid_spec=grid_spec,
        out_shape=jax.ShapeDtypeStruct((m, d), F32),
        compiler_params=_params(2),
        name="attn_sample",
    )(page_table.reshape(-1), q, k, v, *([cache_kt] * pps), *([cache_vt] * pps), bias_rows, tri)


def _conv_prompt_kernel(halo_ref, u_ref, h_ref, wdw_ref, gn_ref, w2_ref, o_ref, ext_ref, y_ref,
                        *, nb, rc):
    i = pl.program_id(0)
    rows = u_ref.shape[0]
    hrows = halo_ref.shape[0]
    width = wdw_ref.shape[0]
    ext_ref[0:hrows, :] = jnp.where(i == 0, 0.0, halo_ref[...])
    ext_ref[hrows:, :] = u_ref[...]

    def chunk(c, carry):
        r0 = pl.multiple_of(c * rc, rc)
        y = jnp.zeros((rc, u_ref.shape[1]), F32)
        for j in range(width):
            start = hrows - (width - 1 - j) * nb
            x = ext_ref[pl.ds(pl.multiple_of(r0 + start, SUBLANES), rc), :]
            y = y + x * wdw_ref[j:j + 1, :]
        y_ref[pl.ds(r0, rc), :] = y
        return carry

    lax.fori_loop(0, rows // rc, chunk, 0)
    yn = _rms(y_ref[...], gn_ref[...])
    o_ref[...] = h_ref[...] + _dot((yn * _sigmoid(yn)).astype(BF16), w2_ref[...])


def _conv_prompt(h, u, wdw, gn, w2, layer, nb):
    m, d = h.shape
    assert nb % SUBLANES == 0, "a time shift must move whole sublane groups"
    hrows = (wdw.shape[0] - 1) * nb
    while m % hrows:
        hrows += SUBLANES
    per = max(k for k in range(1, max(ROW_TILE // hrows, 1) + 1) if m % (k * hrows) == 0)
    tm = per * hrows
    rc = _largest_tile(tm, 4 * SUBLANES)
    tile = pl.BlockSpec((tm, d), lambda i: (i, 0))
    halo = pl.BlockSpec((hrows, d), lambda i: (jnp.maximum(i * per - 1, 0), 0))
    return pl.pallas_call(
        functools.partial(_conv_prompt_kernel, nb=nb, rc=rc),
        grid=(m // tm,),
        in_specs=[halo, tile, tile, _const_spec(wdw.shape), _const_spec((1, d)),
                  _layer_spec(w2, layer)],
        out_specs=tile,
        out_shape=jax.ShapeDtypeStruct((m, d), F32),
        input_output_aliases={2: 0},
        scratch_shapes=[pltpu.VMEM((hrows + tm, d), F32), pltpu.VMEM((tm, d), F32)],
        compiler_params=_params(1),
        name="conv_prompt",
    )(u, u, h, wdw, gn, w2)


def _conv_sample_kernel(st_ref, u_ref, h_ref, wdw_ref, gn_ref, w2_ref, o_ref):
    width = wdw_ref.shape[0]
    ns = st_ref.shape[0]
    for t in range(u_ref.shape[0]):
        y = jnp.zeros(h_ref.shape[1:], F32)
        for j in range(width):
            i = t + j
            x = st_ref[i] if i < ns else u_ref[i - ns]
            y = y + x * wdw_ref[j:j + 1, :]
        yn = _rms(y, gn_ref[...])
        o_ref[t] = h_ref[t] + _dot((yn * _sigmoid(yn)).astype(BF16), w2_ref[...])


def _conv_sample(h_t, u_t, st_t, wdw, gn, w2, layer):
    t, nb, d = h_t.shape
    bs = _largest_tile(nb, 32)
    blk = lambda n: pl.BlockSpec((n, bs, d), lambda i: (0, i, 0))
    return pl.pallas_call(
        _conv_sample_kernel,
        grid=(nb // bs,),
        in_specs=[blk(st_t.shape[0]), blk(t), blk(t), _const_spec(wdw.shape),
                  _const_spec((1, d)), _layer_spec(w2, layer)],
        out_specs=blk(t),
        out_shape=jax.ShapeDtypeStruct((t, nb, d), F32),
        compiler_params=_params(1),
        name="conv_sample",
    )(st_t, u_t, h_t, wdw, gn, w2)


def _ssm_prep_kernel(are_ref, aim_ref, ldt_ref, bre_ref, bim_ref,
                     pwre_ref, pwim_ref, bbre_ref, bbim_ref):
    a_re, a_im = are_ref[...], aim_ref[...]
    dt = jnp.exp(ldt_ref[...])
    mag = jnp.exp(dt * a_re)
    ab_re, ab_im = mag * jnp.cos(dt * a_im), mag * jnp.sin(dt * a_im)
    den = a_re * a_re + a_im * a_im
    nr, ni = ab_re - 1.0, ab_im
    f_re = (nr * a_re + ni * a_im) / den
    f_im = (ni * a_re - nr * a_im) / den
    b_re, b_im = bre_ref[...], bim_ref[...]
    bbre_ref[...] = f_re * b_re - f_im * b_im
    bbim_ref[...] = f_re * b_im + f_im * b_re
    p_re, p_im = ab_re, ab_im
    for k in range(pwre_ref.shape[0]):
        pwre_ref[k:k + 1, :] = p_re
        pwim_ref[k:k + 1, :] = p_im
        p_re, p_im = p_re * ab_re - p_im * ab_im, p_re * ab_im + p_im * ab_re


def _ssm_prep(a_re, a_im, ldt, b_re, b_im):
    gp = a_re.shape[1]
    gc = b_re.shape[0]
    vec = jax.ShapeDtypeStruct((SUBLANES, gp), F32)
    mat = jax.ShapeDtypeStruct((gc, gp), F32)
    return pl.pallas_call(
        _ssm_prep_kernel,
        out_shape=[vec, vec, mat, mat],
        compiler_params=pltpu.CompilerParams(vmem_limit_bytes=VMEM_LIMIT),
        name="ssm_prep",
    )(a_re, a_im, ldt, b_re, b_im)


def _ssm_kernel(*refs, nb, t_last, lc):
    chained = nb > 0
    if chained:
        (h_ref, g_ref, bbre_ref, bbim_ref, ccre_ref, ccim_ref, pwre_ref, pwim_ref, dsk_ref,
         wglu_ref, o_ref, stre_ref, stim_ref, sre, sim, cre, cim) = refs
    else:
        (h_ref, g_ref, bbre_ref, bbim_ref, ccre_ref, ccim_ref, pwre_ref, pwim_ref, dsk_ref,
         wglu_ref, h0re_ref, h0im_ref, o_ref, stre_ref, stim_ref, sre, sim) = refs
    rows, d = h_ref.shape
    gp = sre.shape[1]
    nblk = d // LANES
    cw = gp // nblk

    h = h_ref[...]
    n = _rms(h, g_ref[...])
    nb16 = n.astype(BF16)
    for c in range(nblk):
        x = nb16[:, c * LANES:(c + 1) * LANES]
        sre[:, c * cw:(c + 1) * cw] = _dot(x, bbre_ref[c])
        sim[:, c * cw:(c + 1) * cw] = _dot(x, bbim_ref[c])

    if chained:
        step = pl.program_id(0)

        @pl.when(step == 0)
        def _():
            cre[...] = jnp.zeros(cre.shape, F32)
            cim[...] = jnp.zeros(cim.shape, F32)

        for c in range(gp // lc):
            ls = slice(c * lc, (c + 1) * lc)
            a_re = jnp.broadcast_to(pwre_ref[0:1, ls], (nb, lc))
            a_im = jnp.broadcast_to(pwim_ref[0:1, ls], (nb, lc))

            def body(t, carry, ls=ls, a_re=a_re, a_im=a_im):
                x_re, x_im = carry
                r0 = pl.multiple_of(t * nb, nb)
                x_re, x_im = (a_re * x_re - a_im * x_im + sre[pl.ds(r0, nb), ls],
                              a_re * x_im + a_im * x_re + sim[pl.ds(r0, nb), ls])
                sre[pl.ds(r0, nb), ls] = x_re
                sim[pl.ds(r0, nb), ls] = x_im
                return x_re, x_im

            x_re, x_im = lax.fori_loop(0, rows // nb, body, (cre[:, ls], cim[:, ls]), unroll=8)
            cre[:, ls] = x_re
            cim[:, ls] = x_im

        @pl.when(step == t_last * nb // rows)
        def _():
            r0 = t_last * nb % rows
            stre_ref[...] = sre[r0:r0 + nb, :]
            stim_ref[...] = sim[r0:r0 + nb, :]
    else:
        rowi = lax.broadcasted_iota(jnp.int32, (SUBLANES, lc), 0)
        for c in range(gp // lc):
            ls = slice(c * lc, (c + 1) * lc)
            p_re, p_im = pwre_ref[:, ls], pwim_ref[:, ls]

            def body(r, carry, ls=ls, p_re=p_re, p_im=p_im):
                r0 = pl.multiple_of(r * SUBLANES, SUBLANES)
                c_re, c_im = h0re_ref[pl.ds(r, 1), ls], h0im_ref[pl.ds(r, 1), ls]
                x_re, x_im = sre[pl.ds(r0, SUBLANES), ls], sim[pl.ds(r0, SUBLANES), ls]
                for dist in (1, 2, 4):
                    a_re, a_im = p_re[dist - 1:dist], p_im[dist - 1:dist]
                    s_re = jnp.where(rowi >= dist, pltpu.roll(x_re, dist, axis=0), 0.0)
                    s_im = jnp.where(rowi >= dist, pltpu.roll(x_im, dist, axis=0), 0.0)
                    x_re, x_im = (x_re + a_re * s_re - a_im * s_im,
                                  x_im + a_re * s_im + a_im * s_re)
                x_re, x_im = (x_re + p_re * c_re - p_im * c_im,
                              x_im + p_re * c_im + p_im * c_re)
                sre[pl.ds(r0, SUBLANES), ls] = x_re
                sim[pl.ds(r0, SUBLANES), ls] = x_im
                stre_ref[pl.ds(r, 1), ls] = x_re[SUBLANES - 1:]
                stim_ref[pl.ds(r, 1), ls] = x_im[SUBLANES - 1:]
                return carry

            lax.fori_loop(0, rows // SUBLANES, body, 0)

    hre = sre[...].astype(BF16)
    him = sim[...].astype(BF16)
    y = jnp.concatenate(
        [_dot(hre[:, c * cw:(c + 1) * cw], ccre_ref[c]) - _dot(him[:, c * cw:(c + 1) * cw], ccim_ref[c])
         for c in range(nblk)], axis=1)
    y = (y + dsk_ref[...] * n).astype(BF16)
    za = _dot(y, wglu_ref[:, :d])
    zb = _dot(y, wglu_ref[:, d:])
    o_ref[...] = h + za * _sigmoid(zb)


def _ssm_common_specs(d, bb, cc, pw, wglu, layer):
    return [_const_spec((1, d)), _const_spec(bb.shape), _const_spec(bb.shape),
            _const_spec(cc.shape), _const_spec(cc.shape), _const_spec(pw.shape),
            _const_spec(pw.shape), _const_spec((1, d)), _layer_spec(wglu, layer)]


def _ssm_prompt(h, g, bbre, bbim, ccre, ccim, pwre, pwim, dsk, wglu, layer, nb, t_valid):
    m, d = h.shape
    gp = pwre.shape[1]
    assert nb % SUBLANES == 0, "one time step of all sequences must fill whole sublane groups"
    rows = _largest_tile(m, 256, nb)
    lc = min(SSM_LANE_CHUNK, gp)
    tile = pl.BlockSpec((rows, d), lambda i: (i, 0))
    st = _const_spec((nb, gp))
    st_shape = jax.ShapeDtypeStruct((nb, gp), F32)
    return pl.pallas_call(
        functools.partial(_ssm_kernel, nb=nb, t_last=t_valid - 1, lc=lc),
        grid=(m // rows,),
        in_specs=[tile] + _ssm_common_specs(d, bbre, ccre, pwre, wglu, layer),
        out_specs=[tile, pl.BlockSpec((nb, gp), lambda i: (0, 0)),
                   pl.BlockSpec((nb, gp), lambda i: (0, 0))],
        out_shape=[jax.ShapeDtypeStruct((m, d), F32), st_shape, st_shape],
        input_output_aliases={0: 0},
        scratch_shapes=[pltpu.VMEM((rows, gp), F32), pltpu.VMEM((rows, gp), F32),
                        pltpu.VMEM((nb, gp), F32), pltpu.VMEM((nb, gp), F32)],
        compiler_params=_params(1),
        name="ssm_prompt",
    )(h, g, bbre, bbim, ccre, ccim, pwre, pwim, dsk, wglu)


def _ssm_sample(h, g, bbre, bbim, ccre, ccim, pwre, pwim, dsk, wglu, layer, h0re, h0im, t):
    m, d = h.shape
    assert t == SUBLANES, "the sample scan handles one 8-token block per sequence"
    nb, gp = h0re.shape
    bs = _largest_tile(nb, 32)
    rows = bs * t
    lc = min(SSM_LANE_CHUNK, gp)
    tile = pl.BlockSpec((rows, d), lambda i: (i, 0))
    st = pl.BlockSpec((bs, gp), lambda i: (i, 0))
    st_shape = jax.ShapeDtypeStruct((nb, gp), F32)
    return pl.pallas_call(
        functools.partial(_ssm_kernel, nb=0, t_last=0, lc=lc),
        grid=(nb // bs,),
        in_specs=[tile] + _ssm_common_specs(d, bbre, ccre, pwre, wglu, layer) + [st, st],
        out_specs=[tile, st, st],
        out_shape=[jax.ShapeDtypeStruct((m, d), F32), st_shape, st_shape],
        input_output_aliases={0: 0},
        scratch_shapes=[pltpu.VMEM((rows, gp), F32), pltpu.VMEM((rows, gp), F32)],
        compiler_params=_params(1),
        name="ssm_sample",
    )(h, g, bbre, bbim, ccre, ccim, pwre, pwim, dsk, wglu, h0re, h0im)


def _block_diag_blocks(full, row_group, col_group, row_blk, col_blk):
    rows, cols = full.shape
    keep = (jnp.arange(rows)[:, None] // row_group) == (jnp.arange(cols)[None, :] // col_group)
    full = jnp.where(keep, full, 0.0)
    return jnp.stack([full[i * row_blk:(i + 1) * row_blk, i * col_blk:(i + 1) * col_blk]
                      for i in range(rows // row_blk)]).astype(BF16)


def kernel(x_prompt, x_sample, cache_k, cache_v, page_table, state_conv, state_ssm_re, state_ssm_im, meta_tokens, norm_g, ffn1_w_gate, ffn1_w_up, ffn1_w_down, ffn2_w_gate, ffn2_w_up, ffn2_w_down, sb_w_qkv, sb_w_o, sb_q_norm, sb_k_norm, sb_logit_bias, conv_w_pw1, conv_w_dw, conv_norm_g, conv_w_pw2, ssm_A_re, ssm_A_im, ssm_log_dt, ssm_B_re, ssm_B_im, ssm_C_re, ssm_C_im, ssm_D, ssm_w_glu):
    nb, seq, d = x_prompt.shape
    nsb, ts_s, _ = x_sample.shape
    depth = norm_g.shape[0]
    n_meta = meta_tokens.shape[0]
    nh = sb_logit_bias.shape[1]
    hd = d // nh
    t_valid = seq + n_meta
    t_pad = _round_up(t_valid, SEQ_TILE)
    m_p = nb * t_pad
    m_s = nsb * ts_s
    assert d % LANES == 0 and LANES % hd == 0

    meta = jnp.broadcast_to(meta_tokens[None], (nb, n_meta, d))
    pad = jnp.zeros((nb, t_pad - t_valid, d), F32)
    hp = jnp.concatenate([meta, x_prompt, pad], axis=1).reshape(m_p, d)
    hs = x_sample.reshape(m_s, d)
    time_major = False

    def set_order(x, want_time_major):
        if want_time_major == time_major:
            return x
        a, b = (nb, t_pad) if want_time_major else (t_pad, nb)
        return x.reshape(a, b, d).transpose(1, 0, 2).reshape(m_p, d)

    page = cache_k.shape[2]
    cache_kt = cache_k.transpose(0, 1, 3, 4, 2).reshape(cache_k.shape[:2] + (d, page))
    cache_vt = cache_v.transpose(0, 1, 3, 4, 2).reshape(cache_v.shape[:2] + (d, page))
    idx = jnp.arange(LANES)
    head_ones = (idx[:, None] // hd == idx[None, :] // hd).astype(BF16)

    ffn1 = [w.astype(BF16) for w in (ffn1_w_gate, ffn1_w_up, ffn1_w_down)]
    ffn2 = [w.astype(BF16) for w in (ffn2_w_gate, ffn2_w_up, ffn2_w_down)]
    w_qkv, w_o = sb_w_qkv.astype(BF16), sb_w_o.astype(BF16)
    w_pw1, w_pw2 = conv_w_pw1.astype(BF16), conv_w_pw2.astype(BF16)
    w_glu = ssm_w_glu.astype(BF16)

    new_k, new_v, new_conv, new_re, new_im = [], [], [], [], []
    for i in range(depth):
        kind, j = i % 3, i // 3
        g1, g_mix, g2 = norm_g[i, 0][None], norm_g[i, 1][None], norm_g[i, 2][None]
        hp, hs = _ffn(hp, g1, *ffn1, i), _ffn(hs, g1, *ffn1, i)
        if kind == 0:
            hp = set_order(hp, False)
            time_major = False
            gq, gk = jnp.tile(sb_q_norm[j], nh)[None], jnp.tile(sb_k_norm[j], nh)[None]
            qp, kp, vp = [x.reshape(nb, t_pad, d)
                          for x in _qkv(hp, g_mix, w_qkv, j, gq, gk, head_ones, hd)]
            qs, ks, vs = _qkv(hs, g_mix, w_qkv, j, gq, gk, head_ones, hd)
            bias = sb_logit_bias[j]
            op = _attn_prompt(qp, kp, vp, bias, _suffix_matrix(SEQ_TILE, 1), hd)
            bias_rows = jnp.broadcast_to(jnp.repeat(bias, ts_s)[:, None], (nh * ts_s, page))
            os_ = _attn_sample(qs, ks, vs, cache_kt, cache_vt, j, page_table, bias_rows,
                               _suffix_matrix(page, 2), ts_s, nh, hd)
            hp = _proj_res(hp, op.reshape(m_p, d), w_o, j)
            hs = _proj_res(hs, os_, w_o, j)
            new_k.append((kp[:, :t_valid].reshape(nb, t_valid, nh, hd),
                          ks.reshape(nsb, ts_s, nh, hd)))
            new_v.append((vp[:, :t_valid].reshape(nb, t_valid, nh, hd),
                          vs.reshape(nsb, ts_s, nh, hd)))
        elif kind == 1:
            hp = set_order(hp, True)
            time_major = True
            wdw, gn = conv_w_dw[j], conv_norm_g[j][None]
            width = wdw.shape[0]
            up = _norm_glu(hp, g_mix, w_pw1, j)
            hp = _conv_prompt(hp, up, wdw, gn, w_pw2, j, nb)
            us = _norm_glu(hs, g_mix, w_pw1, j).reshape(nsb, ts_s, d)
            hs_t = _conv_sample(hs.reshape(nsb, ts_s, d).transpose(1, 0, 2), us.transpose(1, 0, 2),
                                state_conv[j].transpose(1, 0, 2), wdw, gn, w_pw2, j)
            hs = hs_t.transpose(1, 0, 2).reshape(m_s, d)
            up_tail = up.reshape(t_pad, nb, d)[t_valid - (width - 1):t_valid]
            new_conv.append((up_tail.transpose(1, 0, 2),
                             jnp.concatenate([state_conv[j], us], axis=1)[:, -(width - 1):]))
        else:
            hp = set_order(hp, True)
            time_major = True
            ng, ns = ssm_A_re.shape[1:]
            gc = ssm_B_re.shape[3]
            gp = ng * ns
            pwre, pwim, bbre, bbim = _ssm_prep(
                ssm_A_re[j].reshape(1, gp), ssm_A_im[j].reshape(1, gp),
                jnp.repeat(ssm_log_dt[j], ns)[None],
                ssm_B_re[j].transpose(2, 0, 1).reshape(gc, gp),
                ssm_B_im[j].transpose(2, 0, 1).reshape(gc, gp))
            gpb = LANES // gc
            bb = [_block_diag_blocks(jnp.tile(x, (ng, 1)), gc, ns, LANES, gpb * ns)
                  for x in (bbre, bbim)]
            cc = [_block_diag_blocks(jnp.tile(x[j].transpose(0, 2, 1).reshape(gp, gc), (1, ng)),
                                     ns, gc, gpb * ns, LANES) for x in (ssm_C_re, ssm_C_im)]
            args = (g_mix, bb[0], bb[1], cc[0], cc[1], pwre, pwim, ssm_D[j][None], w_glu, j)
            hp, pre, pim = _ssm_prompt(hp, *args, nb, t_valid)
            hs, sre, sim = _ssm_sample(hs, *args, state_ssm_re[j].reshape(nsb, gp),
                                       state_ssm_im[j].reshape(nsb, gp), ts_s)
            new_re.append((pre.reshape(nb, ng, ns), sre.reshape(nsb, ng, ns)))
            new_im.append((pim.reshape(nb, ng, ns), sim.reshape(nsb, ng, ns)))
        hp, hs = _ffn(hp, g2, *ffn2, i), _ffn(hs, g2, *ffn2, i)

    def stack(pairs, which):
        return jnp.stack([p[which] for p in pairs])

    hp = set_order(hp, False)
    y_prompt = hp.reshape(nb, t_pad, d)[:, n_meta:t_valid]
    y_sample = hs.reshape(nsb, ts_s, d)
    return (y_prompt, y_sample, stack(new_k, 0), stack(new_v, 0), stack(new_k, 1),
            stack(new_v, 1), stack(new_conv, 0), stack(new_conv, 1), stack(new_re, 0),
            stack(new_im, 0), stack(new_re, 1), stack(new_im, 1))
```

```python
import functools

import jax
import jax.numpy as jnp
from jax import lax
from jax.experimental import pallas as pl
from jax.experimental.pallas import tpu as pltpu

F32 = jnp.float32
BF16 = jnp.bfloat16
EPS = 1e-6
LANES = 128
SUBLANES = 8
VMEM_LIMIT = 56 * 1024 * 1024
SEQ_TILE = 128
Q_TILE = 256
ROW_TILE = 512
SSM_LANE_CHUNK = 512


def _round_up(x, m):
    return (x + m - 1) // m * m


def _largest_tile(n, cap, mult=SUBLANES):
    best = mult
    for t in range(mult, cap + 1, mult):
        if n % t == 0:
            best = t
    return best


def _row_tile(m):
    tm = ROW_TILE
    while m % tm:
        tm //= 2
    assert tm >= SUBLANES
    return tm


def _params(n_grid):
    return pltpu.CompilerParams(dimension_semantics=("arbitrary",) * n_grid,
                                vmem_limit_bytes=VMEM_LIMIT)


def _const_spec(shape):
    nd = len(shape)
    return pl.BlockSpec(shape, lambda *_: (0,) * nd, pipeline_mode=pl.Buffered(1))


def _layer_spec(w, layer):
    nd = w.ndim - 1
    return pl.BlockSpec((None,) + w.shape[1:], lambda *_: (layer,) + (0,) * nd,
                        pipeline_mode=pl.Buffered(1))


def _dot(a, b):
    return jnp.dot(a, b, preferred_element_type=F32)


def _dot_t(a, b):
    return lax.dot_general(a, b, (((1,), (1,)), ((), ())), preferred_element_type=F32)


def _rms(x, g):
    ms = jnp.mean(x * x, axis=-1, keepdims=True)
    return x * lax.rsqrt(ms + EPS) * g


def _sigmoid(x):
    return 1.0 / (1.0 + jnp.exp(-x))


def _ffn_kernel(h_ref, g_ref, wg_ref, wu_ref, wd_ref, o_ref, *, fc):
    h = h_ref[...]
    n = _rms(h, g_ref[...]).astype(BF16)
    acc = jnp.zeros(h.shape, F32)
    for c in range(wg_ref.shape[1] // fc):
        sl = slice(c * fc, (c + 1) * fc)
        gate = _dot(n, wg_ref[:, sl])
        up = _dot(n, wu_ref[:, sl])
        a = (gate * _sigmoid(gate) * up).astype(BF16)
        acc = acc + _dot(a, wd_ref[sl, :])
    o_ref[...] = h + 0.5 * acc


def _ffn(h, g, wg, wu, wd, layer):
    m, d = h.shape
    tm = _row_tile(m)
    f = wg.shape[2]
    fc = 256 if f % 256 == 0 else LANES
    row = pl.BlockSpec((tm, d), lambda i: (i, 0))
    return pl.pallas_call(
        functools.partial(_ffn_kernel, fc=fc),
        grid=(m // tm,),
        in_specs=[row, _const_spec((1, d)), _layer_spec(wg, layer), _layer_spec(wu, layer),
                  _layer_spec(wd, layer)],
        out_specs=row,
        out_shape=jax.ShapeDtypeStruct((m, d), F32),
        input_output_aliases={0: 0},
        compiler_params=_params(1),
        name="ffn",
    )(h, g, wg, wu, wd)


def _proj_res_kernel(h_ref, x_ref, w_ref, o_ref):
    o_ref[...] = h_ref[...] + _dot(x_ref[...].astype(BF16), w_ref[...])


def _proj_res(h, x, w, layer):
    m, d = h.shape
    tm = _row_tile(m)
    row = pl.BlockSpec((tm, d), lambda i: (i, 0))
    return pl.pallas_call(
        _proj_res_kernel,
        grid=(m // tm,),
        in_specs=[row, pl.BlockSpec((tm, x.shape[1]), lambda i: (i, 0)), _layer_spec(w, layer)],
        out_specs=row,
        out_shape=jax.ShapeDtypeStruct((m, d), F32),
        input_output_aliases={0: 0},
        compiler_params=_params(1),
        name="proj_res",
    )(h, x, w)


def _norm_glu_kernel(h_ref, g_ref, w_ref, o_ref):
    d = h_ref.shape[1]
    n = _rms(h_ref[...], g_ref[...]).astype(BF16)
    a = _dot(n, w_ref[:, :d])
    gate = _dot(n, w_ref[:, d:])
    o_ref[...] = a * _sigmoid(gate)


def _norm_glu(h, g, w, layer):
    m, d = h.shape
    tm = _row_tile(m)
    row = pl.BlockSpec((tm, d), lambda i: (i, 0))
    return pl.pallas_call(
        _norm_glu_kernel,
        grid=(m // tm,),
        in_specs=[row, _const_spec((1, d)), _layer_spec(w, layer)],
        out_specs=row,
        out_shape=jax.ShapeDtypeStruct((m, d), F32),
        compiler_params=_params(1),
        name="norm_glu",
    )(h, g, w)


def _qkv_kernel(h_ref, g_ref, w_ref, gq_ref, gk_ref, hm_ref, q_ref, k_ref, v_ref, *, hd):
    d = h_ref.shape[1]
    n = _rms(h_ref[...], g_ref[...]).astype(BF16)
    hm = hm_ref[...]

    def head_norm(x, gain):
        x2 = (x * x).astype(BF16)
        ms = jnp.concatenate(
            [_dot(x2[:, c * LANES:(c + 1) * LANES], hm) for c in range(d // LANES)], axis=1)
        return x * lax.rsqrt(ms * (1.0 / hd) + EPS) * gain

    q = head_norm(_dot(n, w_ref[:, :d]), gq_ref[...])
    q_ref[...] = q * (hd ** -0.5)
    k_ref[...] = head_norm(_dot(n, w_ref[:, d:2 * d]), gk_ref[...])
    v_ref[...] = _dot(n, w_ref[:, 2 * d:])


def _qkv(h, g, w, layer, gq, gk, hm, hd):
    m, d = h.shape
    tm = _row_tile(m)
    row = pl.BlockSpec((tm, d), lambda i: (i, 0))
    out = jax.ShapeDtypeStruct((m, d), F32)
    return pl.pallas_call(
        functools.partial(_qkv_kernel, hd=hd),
        grid=(m // tm,),
        in_specs=[row, _const_spec((1, d)), _layer_spec(w, layer), _const_spec((1, d)),
                  _const_spec((1, d)), _const_spec(hm.shape)],
        out_specs=[row, row, row],
        out_shape=[out, out, out],
        compiler_params=_params(1),
        name="qkv",
    )(h, g, w, gq, gk, hm)


def _softplus(z):
    return jnp.maximum(z, 0.0) + jnp.log(1.0 + jnp.exp(-jnp.abs(z)))


def _hi_lo(x):
    hi = x.astype(BF16)
    lo = (x - hi.astype(F32)).astype(BF16)
    return jnp.concatenate([hi, lo], axis=1)


def _suffix_matrix(n, parts):
    i = jnp.arange(n)
    tri = i[:, None] >= i[None, :]
    return jnp.concatenate([tri] * parts, axis=0).astype(BF16)


def _attn_prompt_kernel(bias_ref, q_ref, k_ref, v_ref, tri_ref, o_ref, k16, vm16, carry_ref,
                        acc_ref, z_ref, sp_ref, *, hd, tk):
    hg = pl.program_id(1)
    qi = pl.program_id(2)
    tq = q_ref.shape[0]
    t_pad = k_ref.shape[0]
    nsub = tq // tk
    hpl = LANES // hd
    nlb = q_ref.shape[1] // LANES
    nh = nlb * hpl
    lane = lax.broadcasted_iota(jnp.int32, (tq, LANES), 1)

    @pl.when(qi == 0)
    def _():
        k16[0:t_pad, :] = k_ref[...].astype(BF16)
        v = v_ref[...]
        lanes = lax.broadcasted_iota(jnp.int32, v.shape, 1) % LANES
        tail = k16.shape[0] - t_pad
        for j in range(hpl):
            vm16[j, 0:t_pad, :] = jnp.where(lanes // hd == j, v, 0.0).astype(BF16)
            if tail:
                vm16[j, t_pad:, :] = jnp.zeros((tail, k16.shape[1]), BF16)
        if tail:
            k16[t_pad:, :] = jnp.zeros((tail, k16.shape[1]), BF16)

    tri = tri_ref[...]
    qm = []
    for p in range(nlb):
        qp = q_ref[:, p * LANES:(p + 1) * LANES]
        qm.append(jnp.concatenate([jnp.where(lane // hd == j, qp, 0.0) for j in range(hpl)],
                                  axis=0).astype(BF16))
    bias = [[bias_ref[(hg * nlb + p) * hpl + j] for j in range(hpl)] for p in range(nlb)]
    carry_ref[...] = jnp.zeros(carry_ref.shape, F32)
    acc_ref[...] = jnp.zeros(acc_ref.shape, F32)

    def diagonal_tile(s):
        k0 = pl.multiple_of((qi * nsub + s) * tk, tk)
        lo = s * tk
        mr = tq - lo
        qpos = qi * tq + lo + lax.broadcasted_iota(jnp.int32, (mr, tk), 0)
        mask = k0 + lax.broadcasted_iota(jnp.int32, (mr, tk), 1) < qpos
        zs, sps = [], []
        for p in range(nlb):
            qrows = jnp.concatenate([qm[p][j * tq + lo:(j + 1) * tq] for j in range(hpl)], axis=0)
            zp = _dot_t(qrows, k16[pl.ds(k0, tk), p * LANES:(p + 1) * LANES])
            for j in range(hpl):
                z = zp[j * mr:(j + 1) * mr] + bias[p][j]
                zs.append(z)
                sps.append(jnp.where(mask, _softplus(z), 0.0))
        suffix = _dot(jnp.concatenate([sp.astype(BF16) for sp in sps], axis=0), tri)
        for p in range(nlb):
            ws = []
            for j in range(hpl):
                i = p * hpl + j
                carry = carry_ref[i, lo:, :]
                w = jnp.exp(zs[i] - suffix[i * mr:(i + 1) * mr] - carry)
                ws.append(jnp.where(mask, w, 0.0).astype(BF16))
                carry_ref[i, lo:, :] = carry + jnp.sum(sps[i], axis=1, keepdims=True)
            vm = jnp.concatenate(
                [vm16[j, pl.ds(k0, tk), p * LANES:(p + 1) * LANES] for j in range(hpl)], axis=0)
            acc_ref[p, lo:, :] += _dot(jnp.concatenate(ws, axis=1), vm)

    for s in reversed(range(nsub)):
        diagonal_tile(s)

    def logits(kt, slot):
        k0 = pl.multiple_of(kt * tk, tk)
        for p in range(nlb):
            zp = _dot_t(qm[p], k16[pl.ds(k0, tk), p * LANES:(p + 1) * LANES])
            for j in range(hpl):
                z = zp[j * tq:(j + 1) * tq] + bias[p][j]
                z_ref[slot, p * hpl + j] = z
                sp_ref[slot, p * hpl + j] = _softplus(z)

    def weigh(kt, slot):
        k0 = pl.multiple_of(kt * tk, tk)
        sps = [sp_ref[slot, i] for i in range(nh)]
        suffix = _dot(jnp.concatenate([sp.astype(BF16) for sp in sps], axis=0), tri)
        for p in range(nlb):
            ws = []
            for j in range(hpl):
                i = p * hpl + j
                carry = carry_ref[i]
                w = jnp.exp(z_ref[slot, i] - suffix[i * tq:(i + 1) * tq] - carry)
                ws.append(w.astype(BF16))
                carry_ref[i] = carry + jnp.sum(sps[i], axis=1, keepdims=True)
            vm = jnp.concatenate(
                [vm16[j, pl.ds(k0, tk), p * LANES:(p + 1) * LANES] for j in range(hpl)], axis=0)
            acc_ref[p] += _dot(jnp.concatenate(ws, axis=1), vm)

    n = qi * nsub

    @pl.when(n > 0)
    def _():
        logits(n - 1, 0)

    def body(i, c):
        kt = n - 1 - i
        weigh(kt, i % 2)
        logits(kt - 1, (i + 1) % 2)
        return c

    lax.fori_loop(0, n - 1, body, 0)

    @pl.when(n > 0)
    def _():
        weigh(0, (n - 1) % 2)

    o_ref[...] = jnp.concatenate([acc_ref[p] for p in range(nlb)], axis=1)


def _attn_prompt(q, k, v, bias, tri, hd):
    nb, t_pad, d = q.shape
    tk = SEQ_TILE
    tq = Q_TILE
    nqt = pl.cdiv(t_pad, tq)
    hpl = LANES // hd
    wg = min(d, 4 * LANES)
    assert d % wg == 0 and tq % tk == 0 and t_pad % tk == 0
    seq = pl.BlockSpec((None, t_pad, wg), lambda b, hg, qi: (b, 0, hg))
    tile = pl.BlockSpec((None, tq, wg), lambda b, hg, qi: (b, qi, hg))
    return pl.pallas_call(
        functools.partial(_attn_prompt_kernel, hd=hd, tk=tk),
        grid=(nb, d // wg, nqt),
        in_specs=[pl.BlockSpec(memory_space=pltpu.SMEM), tile, seq, seq, _const_spec(tri.shape)],
        out_specs=tile,
        out_shape=jax.ShapeDtypeStruct((nb, t_pad, d), F32),
        scratch_shapes=[pltpu.VMEM((nqt * tq, wg), BF16), pltpu.VMEM((hpl, nqt * tq, wg), BF16),
                        pltpu.VMEM((wg // hd, tq, tk), F32),
                        pltpu.VMEM((wg // LANES, tq, LANES), F32),
                        pltpu.VMEM((2, wg // hd, tq, tk), F32),
                        pltpu.VMEM((2, wg // hd, tq, tk), F32)],
        compiler_params=_params(3),
        name="attn_prompt",
    )(bias, q, k, v, tri)


def _attn_sample_kernel(pt_ref, q_ref, kn_ref, vn_ref, *refs, nh, hd, pps, n_steps):
    del pt_ref
    kc_refs, vc_refs = refs[:pps], refs[pps:2 * pps]
    bias_ref, tri_ref, o_ref, qbd_ref, carry_ref, acc_ref = refs[2 * pps:]
    s = pl.program_id(1)
    tq, d = q_ref.shape
    tk = kc_refs[0].shape[1]
    r = nh * tq
    hpl = LANES // hd
    rp = hpl * tq
    nlb = d // LANES
    kc = min(d, 2 * LANES)
    rk = kc // hd * tq
    tri = tri_ref[...]
    bias = bias_ref[...]

    @pl.when(s == 0)
    def _():
        row_h = lax.broadcasted_iota(jnp.int32, (r, d), 0) // tq
        lane_h = lax.broadcasted_iota(jnp.int32, (r, d), 1) // hd
        qbd = jnp.where(row_h == lane_h, jnp.tile(q_ref[...], (nh, 1)), 0.0).astype(BF16)
        qbd_ref[...] = qbd
        pad = jnp.zeros((tk - tq, d), F32)
        kb = jnp.concatenate([kn_ref[...], pad], axis=0).astype(BF16)
        vb = jnp.concatenate([vn_ref[...], pad], axis=0).astype(BF16)
        qpos = lax.broadcasted_iota(jnp.int32, (r, tk), 0) % tq
        kpos = lax.broadcasted_iota(jnp.int32, (r, tk), 1)
        mask = kpos < qpos
        z = _dot_t(qbd, kb) + bias
        sp = jnp.where(mask, _softplus(z), 0.0)
        w = jnp.where(mask, jnp.exp(z - _dot(_hi_lo(sp), tri)), 0.0).astype(BF16)
        carry_ref[...] = jnp.zeros((r, tk), F32) + jnp.sum(sp, axis=1, keepdims=True)
        pv = _dot(w, vb)
        acc_ref[...] = jnp.concatenate(
            [pv[c * rp:(c + 1) * rp, c * LANES:(c + 1) * LANES] for c in range(nlb)], axis=0)

    kcat = jnp.concatenate([kc_refs[i][...].astype(BF16) for i in range(pps)], axis=1)
    zcat = jnp.concatenate(
        [_dot(qbd_ref[c * rk:(c + 1) * rk, c * kc:(c + 1) * kc], kcat[c * kc:(c + 1) * kc, :])
         for c in range(d // kc)], axis=0)
    zs = [zcat[:, i * tk:(i + 1) * tk] + bias for i in range(pps)]
    sps = [_softplus(z) for z in zs]
    suffix = _dot(jnp.concatenate([_hi_lo(sp) for sp in sps], axis=0), tri)
    carry = carry_ref[...]
    ws = []
    for i in range(pps):
        ws.append(jnp.exp(zs[i] - suffix[i * r:(i + 1) * r] - carry).astype(BF16))
        carry = carry + jnp.sum(sps[i], axis=1, keepdims=True)
    carry_ref[...] = carry
    wcat = jnp.concatenate(ws, axis=1)
    vcat = jnp.concatenate([vc_refs[i][...].astype(BF16) for i in range(pps)], axis=1)
    acc_ref[...] += jnp.concatenate(
        [_dot_t(wcat[c * rp:(c + 1) * rp, :], vcat[c * LANES:(c + 1) * LANES, :])
         for c in range(nlb)], axis=0)

    @pl.when(s == n_steps - 1)
    def _():
        lane_h = lax.broadcasted_iota(jnp.int32, (tq, LANES), 1) // hd
        cols = []
        for c in range(nlb):
            o = acc_ref[c * rp:c * rp + tq, :]
            for j in range(1, hpl):
                o = jnp.where(lane_h == j, acc_ref[c * rp + j * tq:c * rp + (j + 1) * tq, :], o)
            cols.append(o)
        o_ref[...] = jnp.concatenate(cols, axis=1)


def _attn_sample(q, k, v, cache_kt, cache_vt, layer, page_table, bias_rows, tri, tq, nh, hd):
    m, d = q.shape
    nb, n_pages = page_table.shape
    tk = cache_kt.shape[3]
    r = nh * tq
    pps = max(p for p in range(1, 9) if n_pages % p == 0)
    n_steps = n_pages // pps
    new_rows = pl.BlockSpec((tq, d), lambda b, s, pt: (b, 0))

    def page_spec(i):
        def page_map(b, s, pt):
            return (layer, pt[b * n_pages + n_pages - 1 - (s * pps + i)], 0, 0)
        return pl.BlockSpec((None, None, d, tk), page_map)

    pages = [page_spec(i) for i in range(pps)]
    grid_spec = pltpu.PrefetchScalarGridSpec(
        num_scalar_prefetch=1,
        grid=(nb, n_steps),
        in_specs=[new_rows, new_rows, new_rows] + pages + pages
        + [_const_spec(bias_rows.shape), _const_spec(tri.shape)],
        out_specs=new_rows,
        scratch_shapes=[pltpu.VMEM((r, d), BF16), pltpu.VMEM((r, tk), F32),
                        pltpu.VMEM((r, LANES), F32)],
    )
    return pl.pallas_call(
        functools.partial(_attn_sample_kernel, nh=nh, hd=hd, pps=pps, n_steps=n_steps),
        grid_spec=grid_spec,
        out_shape=jax.ShapeDtypeStruct((m, d), F32),
        compiler_params=_params(2),
        name="attn_sample",
    )(page_table.reshape(-1), q, k, v, *([cache_kt] * pps), *([cache_vt] * pps), bias_rows, tri)


def _conv_prompt_kernel(halo_ref, u_ref, h_ref, wdw_ref, gn_ref, w2_ref, o_ref, ext_ref, y_ref,
                        *, nb, rc):
    i = pl.program_id(0)
    rows = u_ref.shape[0]
    hrows = halo_ref.shape[0]
    width = wdw_ref.shape[0]
    ext_ref[0:hrows, :] = jnp.where(i == 0, 0.0, halo_ref[...])
    ext_ref[hrows:, :] = u_ref[...]

    def chunk(c, carry):
        r0 = pl.multiple_of(c * rc, rc)
        y = jnp.zeros((rc, u_ref.shape[1]), F32)
        for j in range(width):
            start = hrows - (width - 1 - j) * nb
            x = ext_ref[pl.ds(pl.multiple_of(r0 + start, SUBLANES), rc), :]
            y = y + x * wdw_ref[j:j + 1, :]
        y_ref[pl.ds(r0, rc), :] = y
        return carry

    lax.fori_loop(0, rows // rc, chunk, 0)
    yn = _rms(y_ref[...], gn_ref[...])
    o_ref[...] = h_ref[...] + _dot((yn * _sigmoid(yn)).astype(BF16), w2_ref[...])


def _conv_prompt(h, u, wdw, gn, w2, layer, nb):
    m, d = h.shape
    assert nb % SUBLANES == 0, "a time shift must move whole sublane groups"
    hrows = (wdw.shape[0] - 1) * nb
    while m % hrows:
        hrows += SUBLANES
    per = max(k for k in range(1, max(ROW_TILE // hrows, 1) + 1) if m % (k * hrows) == 0)
    tm = per * hrows
    rc = _largest_tile(tm, 4 * SUBLANES)
    tile = pl.BlockSpec((tm, d), lambda i: (i, 0))
    halo = pl.BlockSpec((hrows, d), lambda i: (jnp.maximum(i * per - 1, 0), 0))
    return pl.pallas_call(
        functools.partial(_conv_prompt_kernel, nb=nb, rc=rc),
        grid=(m // tm,),
        in_specs=[halo, tile, tile, _const_spec(wdw.shape), _const_spec((1, d)),
                  _layer_spec(w2, layer)],
        out_specs=tile,
        out_shape=jax.ShapeDtypeStruct((m, d), F32),
        input_output_aliases={2: 0},
        scratch_shapes=[pltpu.VMEM((hrows + tm, d), F32), pltpu.VMEM((tm, d), F32)],
        compiler_params=_params(1),
        name="conv_prompt",
    )(u, u, h, wdw, gn, w2)


def _conv_sample_kernel(st_ref, u_ref, h_ref, wdw_ref, gn_ref, w2_ref, o_ref):
    width = wdw_ref.shape[0]
    ns = st_ref.shape[0]
    for t in range(u_ref.shape[0]):
        y = jnp.zeros(h_ref.shape[1:], F32)
        for j in range(width):
            i = t + j
            x = st_ref[i] if i < ns else u_ref[i - ns]
            y = y + x * wdw_ref[j:j + 1, :]
        yn = _rms(y, gn_ref[...])
        o_ref[t] = h_ref[t] + _dot((yn * _sigmoid(yn)).astype(BF16), w2_ref[...])


def _conv_sample(h_t, u_t, st_t, wdw, gn, w2, layer):
    t, nb, d = h_t.shape
    bs = _largest_tile(nb, 32)
    blk = lambda n: pl.BlockSpec((n, bs, d), lambda i: (0, i, 0))
    return pl.pallas_call(
        _conv_sample_kernel,
        grid=(nb // bs,),
        in_specs=[blk(st_t.shape[0]), blk(t), blk(t), _const_spec(wdw.shape),
                  _const_spec((1, d)), _layer_spec(w2, layer)],
        out_specs=blk(t),
        out_shape=jax.ShapeDtypeStruct((t, nb, d), F32),
        compiler_params=_params(1),
        name="conv_sample",
    )(st_t, u_t, h_t, wdw, gn, w2)


def _ssm_prep_kernel(are_ref, aim_ref, ldt_ref, bre_ref, bim_ref,
                     pwre_ref, pwim_ref, bbre_ref, bbim_ref):
    a_re, a_im = are_ref[...], aim_ref[...]
    dt = jnp.exp(ldt_ref[...])
    mag = jnp.exp(dt * a_re)
    ab_re, ab_im = mag * jnp.cos(dt * a_im), mag * jnp.sin(dt * a_im)
    den = a_re * a_re + a_im * a_im
    nr, ni = ab_re - 1.0, ab_im
    f_re = (nr * a_re + ni * a_im) / den
    f_im = (ni * a_re - nr * a_im) / den
    b_re, b_im = bre_ref[...], bim_ref[...]
    bbre_ref[...] = f_re * b_re - f_im * b_im
    bbim_ref[...] = f_re * b_im + f_im * b_re
    p_re, p_im = ab_re, ab_im
    for k in range(pwre_ref.shape[0]):
        pwre_ref[k:k + 1, :] = p_re
        pwim_ref[k:k + 1, :] = p_im
        p_re, p_im = p_re * ab_re - p_im * ab_im, p_re * ab_im + p_im * ab_re


def _ssm_prep(a_re, a_im, ldt, b_re, b_im):
    gp = a_re.shape[1]
    gc = b_re.shape[0]
    vec = jax.ShapeDtypeStruct((SUBLANES, gp), F32)
    mat = jax.ShapeDtypeStruct((gc, gp), F32)
    return pl.pallas_call(
        _ssm_prep_kernel,
        out_shape=[vec, vec, mat, mat],
        compiler_params=pltpu.CompilerParams(vmem_limit_bytes=VMEM_LIMIT),
        name="ssm_prep",
    )(a_re, a_im, ldt, b_re, b_im)


def _ssm_kernel(*refs, nb, t_last, lc):
    chained = nb > 0
    if chained:
        (h_ref, g_ref, bbre_ref, bbim_ref, ccre_ref, ccim_ref, pwre_ref, pwim_ref, dsk_ref,
         wglu_ref, o_ref, stre_ref, stim_ref, sre, sim, cre, cim) = refs
    else:
        (h_ref, g_ref, bbre_ref, bbim_ref, ccre_ref, ccim_ref, pwre_ref, pwim_ref, dsk_ref,
         wglu_ref, h0re_ref, h0im_ref, o_ref, stre_ref, stim_ref, sre, sim) = refs
    rows, d = h_ref.shape
    gp = sre.shape[1]
    nblk = d // LANES
    cw = gp // nblk

    h = h_ref[...]
    n = _rms(h, g_ref[...])
    nb16 = n.astype(BF16)
    for c in range(nblk):
        x = nb16[:, c * LANES:(c + 1) * LANES]
        sre[:, c * cw:(c + 1) * cw] = _dot(x, bbre_ref[c])
        sim[:, c * cw:(c + 1) * cw] = _dot(x, bbim_ref[c])

    if chained:
        step = pl.program_id(0)

        @pl.when(step == 0)
        def _():
            cre[...] = jnp.zeros(cre.shape, F32)
            cim[...] = jnp.zeros(cim.shape, F32)

        for c in range(gp // lc):
            ls = slice(c * lc, (c + 1) * lc)
            a_re = jnp.broadcast_to(pwre_ref[0:1, ls], (nb, lc))
            a_im = jnp.broadcast_to(pwim_ref[0:1, ls], (nb, lc))

            def body(t, carry, ls=ls, a_re=a_re, a_im=a_im):
                x_re, x_im = carry
                r0 = pl.multiple_of(t * nb, nb)
                x_re, x_im = (a_re * x_re - a_im * x_im + sre[pl.ds(r0, nb), ls],
                              a_re * x_im + a_im * x_re + sim[pl.ds(r0, nb), ls])
                sre[pl.ds(r0, nb), ls] = x_re
                sim[pl.ds(r0, nb), ls] = x_im
                return x_re, x_im

            x_re, x_im = lax.fori_loop(0, rows // nb, body, (cre[:, ls], cim[:, ls]), unroll=8)
            cre[:, ls] = x_re
            cim[:, ls] = x_im

        @pl.when(step == t_last * nb // rows)
        def _():
            r0 = t_last * nb % rows
            stre_ref[...] = sre[r0:r0 + nb, :]
            stim_ref[...] = sim[r0:r0 + nb, :]
    else:
        rowi = lax.broadcasted_iota(jnp.int32, (SUBLANES, lc), 0)
        for c in range(gp // lc):
            ls = slice(c * lc, (c + 1) * lc)
            p_re, p_im = pwre_ref[:, ls], pwim_ref[:, ls]

            def body(r, carry, ls=ls, p_re=p_re, p_im=p_im):
                r0 = pl.multiple_of(r * SUBLANES, SUBLANES)
                c_re, c_im = h0re_ref[pl.ds(r, 1), ls], h0im_ref[pl.ds(r, 1), ls]
                x_re, x_im = sre[pl.ds(r0, SUBLANES), ls], sim[pl.ds(r0, SUBLANES), ls]
                for dist in (1, 2, 4):
                    a_re, a_im = p_re[dist - 1:dist], p_im[dist - 1:dist]
                    s_re = jnp.where(rowi >= dist, pltpu.roll(x_re, dist, axis=0), 0.0)
                    s_im = jnp.where(rowi >= dist, pltpu.roll(x_im, dist, axis=0), 0.0)
                    x_re, x_im = (x_re + a_re * s_re - a_im * s_im,
                                  x_im + a_re * s_im + a_im * s_re)
                x_re, x_im = (x_re + p_re * c_re - p_im * c_im,
                              x_im + p_re * c_im + p_im * c_re)
                sre[pl.ds(r0, SUBLANES), ls] = x_re
                sim[pl.ds(r0, SUBLANES), ls] = x_im
                stre_ref[pl.ds(r, 1), ls] = x_re[SUBLANES - 1:]
                stim_ref[pl.ds(r, 1), ls] = x_im[SUBLANES - 1:]
                return carry

            lax.fori_loop(0, rows // SUBLANES, body, 0)

    hre = sre[...].astype(BF16)
    him = sim[...].astype(BF16)
    y = jnp.concatenate(
        [_dot(hre[:, c * cw:(c + 1) * cw], ccre_ref[c]) - _dot(him[:, c * cw:(c + 1) * cw], ccim_ref[c])
         for c in range(nblk)], axis=1)
    y = (y + dsk_ref[...] * n).astype(BF16)
    za = _dot(y, wglu_ref[:, :d])
    zb = _dot(y, wglu_ref[:, d:])
    o_ref[...] = h + za * _sigmoid(zb)


def _ssm_common_specs(d, bb, cc, pw, wglu, layer):
    return [_const_spec((1, d)), _const_spec(bb.shape), _const_spec(bb.shape),
            _const_spec(cc.shape), _const_spec(cc.shape), _const_spec(pw.shape),
            _const_spec(pw.shape), _const_spec((1, d)), _layer_spec(wglu, layer)]


def _ssm_prompt(h, g, bbre, bbim, ccre, ccim, pwre, pwim, dsk, wglu, layer, nb, t_valid):
    m, d = h.shape
    gp = pwre.shape[1]
    assert nb % SUBLANES == 0, "one time step of all sequences must fill whole sublane groups"
    rows = _largest_tile(m, 256, nb)
    lc = min(SSM_LANE_CHUNK, gp)
    tile = pl.BlockSpec((rows, d), lambda i: (i, 0))
    st = _const_spec((nb, gp))
    st_shape = jax.ShapeDtypeStruct((nb, gp), F32)
    return pl.pallas_call(
        functools.partial(_ssm_kernel, nb=nb, t_last=t_valid - 1, lc=lc),
        grid=(m // rows,),
        in_specs=[tile] + _ssm_common_specs(d, bbre, ccre, pwre, wglu, layer),
        out_specs=[tile, pl.BlockSpec((nb, gp), lambda i: (0, 0)),
                   pl.BlockSpec((nb, gp), lambda i: (0, 0))],
        out_shape=[jax.ShapeDtypeStruct((m, d), F32), st_shape, st_shape],
        input_output_aliases={0: 0},
        scratch_shapes=[pltpu.VMEM((rows, gp), F32), pltpu.VMEM((rows, gp), F32),
                        pltpu.VMEM((nb, gp), F32), pltpu.VMEM((nb, gp), F32)],
        compiler_params=_params(1),
        name="ssm_prompt",
    )(h, g, bbre, bbim, ccre, ccim, pwre, pwim, dsk, wglu)


def _ssm_sample(h, g, bbre, bbim, ccre, ccim, pwre, pwim, dsk, wglu, layer, h0re, h0im, t):
    m, d = h.shape
    assert t == SUBLANES, "the sample scan handles one 8-token block per sequence"
    nb, gp = h0re.shape
    bs = _largest_tile(nb, 32)
    rows = bs * t
    lc = min(SSM_LANE_CHUNK, gp)
    tile = pl.BlockSpec((rows, d), lambda i: (i, 0))
    st = pl.BlockSpec((bs, gp), lambda i: (i, 0))
    st_shape = jax.ShapeDtypeStruct((nb, gp), F32)
    return pl.pallas_call(
        functools.partial(_ssm_kernel, nb=0, t_last=0, lc=lc),
        grid=(nb // bs,),
        in_specs=[tile] + _ssm_common_specs(d, bbre, ccre, pwre, wglu, layer) + [st, st],
        out_specs=[tile, st, st],
        out_shape=[jax.ShapeDtypeStruct((m, d), F32), st_shape, st_shape],
        input_output_aliases={0: 0},
        scratch_shapes=[pltpu.VMEM((rows, gp), F32), pltpu.VMEM((rows, gp), F32)],
        compiler_params=_params(1),
        name="ssm_sample",
    )(h, g, bbre, bbim, ccre, ccim, pwre, pwim, dsk, wglu, h0re, h0im)


def _block_diag_blocks(full, row_group, col_group, row_blk, col_blk):
    rows, cols = full.shape
    keep = (jnp.arange(rows)[:, None] // row_group) == (jnp.arange(cols)[None, :] // col_group)
    full = jnp.where(keep, full, 0.0)
    return jnp.stack([full[i * row_blk:(i + 1) * row_blk, i * col_blk:(i + 1) * col_blk]
                      for i in range(rows // row_blk)]).astype(BF16)


def kernel(x_prompt, x_sample, cache_k, cache_v, page_table, state_conv, state_ssm_re, state_ssm_im, meta_tokens, norm_g, ffn1_w_gate, ffn1_w_up, ffn1_w_down, ffn2_w_gate, ffn2_w_up, ffn2_w_down, sb_w_qkv, sb_w_o, sb_q_norm, sb_k_norm, sb_logit_bias, conv_w_pw1, conv_w_dw, conv_norm_g, conv_w_pw2, ssm_A_re, ssm_A_im, ssm_log_dt, ssm_B_re, ssm_B_im, ssm_C_re, ssm_C_im, ssm_D, ssm_w_glu):
    nb, seq, d = x_prompt.shape
    nsb, ts_s, _ = x_sample.shape
    depth = norm_g.shape[0]
    n_meta = meta_tokens.shape[0]
    nh = sb_logit_bias.shape[1]
    hd = d // nh
    t_valid = seq + n_meta
    t_pad = _round_up(t_valid, SEQ_TILE)
    m_p = nb * t_pad
    m_s = nsb * ts_s
    assert d % LANES == 0 and LANES % hd == 0

    meta = jnp.broadcast_to(meta_tokens[None], (nb, n_meta, d))
    pad = jnp.zeros((nb, t_pad - t_valid, d), F32)
    hp = jnp.concatenate([meta, x_prompt, pad], axis=1).reshape(m_p, d)
    hs = x_sample.reshape(m_s, d)
    time_major = False

    def set_order(x, want_time_major):
        if want_time_major == time_major:
            return x
        a, b = (nb, t_pad) if want_time_major else (t_pad, nb)
        return x.reshape(a, b, d).transpose(1, 0, 2).reshape(m_p, d)

    page = cache_k.shape[2]
    cache_kt = cache_k.transpose(0, 1, 3, 4, 2).reshape(cache_k.shape[:2] + (d, page))
    cache_vt = cache_v.transpose(0, 1, 3, 4, 2).reshape(cache_v.shape[:2] + (d, page))
    idx = jnp.arange(LANES)
    head_ones = (idx[:, None] // hd == idx[None, :] // hd).astype(BF16)

    ffn1 = [w.astype(BF16) for w in (ffn1_w_gate, ffn1_w_up, ffn1_w_down)]
    ffn2 = [w.astype(BF16) for w in (ffn2_w_gate, ffn2_w_up, ffn2_w_down)]
    w_qkv, w_o = sb_w_qkv.astype(BF16), sb_w_o.astype(BF16)
    w_pw1, w_pw2 = conv_w_pw1.astype(BF16), conv_w_pw2.astype(BF16)
    w_glu = ssm_w_glu.astype(BF16)

    new_k, new_v, new_conv, new_re, new_im = [], [], [], [], []
    for i in range(depth):
        kind, j = i % 3, i // 3
        g1, g_mix, g2 = norm_g[i, 0][None], norm_g[i, 1][None], norm_g[i, 2][None]
        hp, hs = _ffn(hp, g1, *ffn1, i), _ffn(hs, g1, *ffn1, i)
        if kind == 0:
            hp = set_order(hp, False)
            time_major = False
            gq, gk = jnp.tile(sb_q_norm[j], nh)[None], jnp.tile(sb_k_norm[j], nh)[None]
            qp, kp, vp = [x.reshape(nb, t_pad, d)
                          for x in _qkv(hp, g_mix, w_qkv, j, gq, gk, head_ones, hd)]
            qs, ks, vs = _qkv(hs, g_mix, w_qkv, j, gq, gk, head_ones, hd)
            bias = sb_logit_bias[j]
            op = _attn_prompt(qp, kp, vp, bias, _suffix_matrix(SEQ_TILE, 1), hd)
            bias_rows = jnp.broadcast_to(jnp.repeat(bias, ts_s)[:, None], (nh * ts_s, page))
            os_ = _attn_sample(qs, ks, vs, cache_kt, cache_vt, j, page_table, bias_rows,
                               _suffix_matrix(page, 2), ts_s, nh, hd)
            hp = _proj_res(hp, op.reshape(m_p, d), w_o, j)
            hs = _proj_res(hs, os_, w_o, j)
            new_k.append((kp[:, :t_valid].reshape(nb, t_valid, nh, hd),
                          ks.reshape(nsb, ts_s, nh, hd)))
            new_v.append((vp[:, :t_valid].reshape(nb, t_valid, nh, hd),
                          vs.reshape(nsb, ts_s, nh, hd)))
        elif kind == 1:
            hp = set_order(hp, True)
            time_major = True
            wdw, gn = conv_w_dw[j], conv_norm_g[j][None]
            width = wdw.shape[0]
            up = _norm_glu(hp, g_mix, w_pw1, j)
            hp = _conv_prompt(hp, up, wdw, gn, w_pw2, j, nb)
            us = _norm_glu(hs, g_mix, w_pw1, j).reshape(nsb, ts_s, d)
            hs_t = _conv_sample(hs.reshape(nsb, ts_s, d).transpose(1, 0, 2), us.transpose(1, 0, 2),
                                state_conv[j].transpose(1, 0, 2), wdw, gn, w_pw2, j)
            hs = hs_t.transpose(1, 0, 2).reshape(m_s, d)
            up_tail = up.reshape(t_pad, nb, d)[t_valid - (width - 1):t_valid]
            new_conv.append((up_tail.transpose(1, 0, 2),
                             jnp.concatenate([state_conv[j], us], axis=1)[:, -(width - 1):]))
        else:
            hp = set_order(hp, True)
            time_major = True
            ng, ns = ssm_A_re.shape[1:]
            gc = ssm_B_re.shape[3]
            gp = ng * ns
            pwre, pwim, bbre, bbim = _ssm_prep(
                ssm_A_re[j].reshape(1, gp), ssm_A_im[j].reshape(1, gp),
                jnp.repeat(ssm_log_dt[j], ns)[None],
                ssm_B_re[j].transpose(2, 0, 1).reshape(gc, gp),
                ssm_B_im[j].transpose(2, 0, 1).reshape(gc, gp))
            gpb = LANES // gc
            bb = [_block_diag_blocks(jnp.tile(x, (ng, 1)), gc, ns, LANES, gpb * ns)
                  for x in (bbre, bbim)]
            cc = [_block_diag_blocks(jnp.tile(x[j].transpose(0, 2, 1).reshape(gp, gc), (1, ng)),
                                     ns, gc, gpb * ns, LANES) for x in (ssm_C_re, ssm_C_im)]
            args = (g_mix, bb[0], bb[1], cc[0], cc[1], pwre, pwim, ssm_D[j][None], w_glu, j)
            hp, pre, pim = _ssm_prompt(hp, *args, nb, t_valid)
            hs, sre, sim = _ssm_sample(hs, *args, state_ssm_re[j].reshape(nsb, gp),
                                       state_ssm_im[j].reshape(nsb, gp), ts_s)
            new_re.append((pre.reshape(nb, ng, ns), sre.reshape(nsb, ng, ns)))
            new_im.append((pim.reshape(nb, ng, ns), sim.reshape(nsb, ng, ns)))
        hp, hs = _ffn(hp, g2, *ffn2, i), _ffn(hs, g2, *ffn2, i)

    def stack(pairs, which):
        return jnp.stack([p[which] for p in pairs])

    hp = set_order(hp, False)
    y_prompt = hp.reshape(nb, t_pad, d)[:, n_meta:t_valid]
    y_sample = hs.reshape(nsb, ts_s, d)
    return (y_prompt, y_sample, stack(new_k, 0), stack(new_v, 0), stack(new_k, 1),
            stack(new_v, 1), stack(new_conv, 0), stack(new_conv, 1), stack(new_re, 0),
            stack(new_im, 0), stack(new_re, 1), stack(new_im, 1))
```

```python
import functools

import jax
import jax.numpy as jnp
from jax import lax
from jax.experimental import pallas as pl
from jax.experimental.pallas import tpu as pltpu

F32 = jnp.float32
BF16 = jnp.bfloat16
EPS = 1e-6
LANES = 128
SUBLANES = 8
VMEM_LIMIT = 56 * 1024 * 1024
SEQ_TILE = 128
SEQ_PAD = 16
Q_TILE = 256
ROW_TILE = 512
SSM_LANE_CHUNK = 512


def _round_up(x, m):
    return (x + m - 1) // m * m


def _largest_tile(n, cap, mult=SUBLANES):
    best = mult
    for t in range(mult, cap + 1, mult):
        if n % t == 0:
            best = t
    return best


def _row_tile(m):
    tm = _largest_tile(m, ROW_TILE, LANES)
    return tm if m % tm == 0 else _largest_tile(m, ROW_TILE)


def _params(n_grid):
    return pltpu.CompilerParams(dimension_semantics=("arbitrary",) * n_grid,
                                vmem_limit_bytes=VMEM_LIMIT)


def _const_spec(shape):
    nd = len(shape)
    return pl.BlockSpec(shape, lambda *_: (0,) * nd, pipeline_mode=pl.Buffered(1))


def _layer_spec(w, layer):
    nd = w.ndim - 1
    return pl.BlockSpec((None,) + w.shape[1:], lambda *_: (layer,) + (0,) * nd,
                        pipeline_mode=pl.Buffered(1))


def _dot(a, b):
    return jnp.dot(a, b, preferred_element_type=F32)


def _dot_t(a, b):
    return lax.dot_general(a, b, (((1,), (1,)), ((), ())), preferred_element_type=F32)


def _rms(x, g):
    ms = jnp.mean(x * x, axis=-1, keepdims=True)
    return x * lax.rsqrt(ms + EPS) * g


def _sigmoid(x):
    return 1.0 / (1.0 + jnp.exp(-x))


def _ffn_kernel(h_ref, g_ref, wg_ref, wu_ref, wd_ref, o_ref, *, fc):
    h = h_ref[...]
    n = _rms(h, g_ref[...]).astype(BF16)
    acc = jnp.zeros(h.shape, F32)
    for c in range(wg_ref.shape[1] // fc):
        sl = slice(c * fc, (c + 1) * fc)
        gate = _dot(n, wg_ref[:, sl])
        up = _dot(n, wu_ref[:, sl])
        a = (gate * _sigmoid(gate) * up).astype(BF16)
        acc = acc + _dot(a, wd_ref[sl, :])
    o_ref[...] = h + 0.5 * acc


def _ffn(h, g, wg, wu, wd, layer):
    m, d = h.shape
    tm = _row_tile(m)
    f = wg.shape[2]
    fc = 256 if f % 256 == 0 else LANES
    row = pl.BlockSpec((tm, d), lambda i: (i, 0))
    return pl.pallas_call(
        functools.partial(_ffn_kernel, fc=fc),
        grid=(m // tm,),
        in_specs=[row, _const_spec((1, d)), _layer_spec(wg, layer), _layer_spec(wu, layer),
                  _layer_spec(wd, layer)],
        out_specs=row,
        out_shape=jax.ShapeDtypeStruct((m, d), F32),
        input_output_aliases={0: 0},
        compiler_params=_params(1),
        name="ffn",
    )(h, g, wg, wu, wd)


def _proj_res_kernel(h_ref, x_ref, w_ref, o_ref):
    o_ref[...] = h_ref[...] + _dot(x_ref[...].astype(BF16), w_ref[...])


def _proj_res(h, x, w, layer):
    m, d = h.shape
    tm = _row_tile(m)
    row = pl.BlockSpec((tm, d), lambda i: (i, 0))
    return pl.pallas_call(
        _proj_res_kernel,
        grid=(m // tm,),
        in_specs=[row, pl.BlockSpec((tm, x.shape[1]), lambda i: (i, 0)), _layer_spec(w, layer)],
        out_specs=row,
        out_shape=jax.ShapeDtypeStruct((m, d), F32),
        input_output_aliases={0: 0},
        compiler_params=_params(1),
        name="proj_res",
    )(h, x, w)


def _norm_glu_kernel(h_ref, g_ref, w_ref, o_ref):
    d = h_ref.shape[1]
    n = _rms(h_ref[...], g_ref[...]).astype(BF16)
    a = _dot(n, w_ref[:, :d])
    gate = _dot(n, w_ref[:, d:])
    o_ref[...] = a * _sigmoid(gate)


def _norm_glu(h, g, w, layer):
    m, d = h.shape
    tm = _row_tile(m)
    row = pl.BlockSpec((tm, d), lambda i: (i, 0))
    return pl.pallas_call(
        _norm_glu_kernel,
        grid=(m // tm,),
        in_specs=[row, _const_spec((1, d)), _layer_spec(w, layer)],
        out_specs=row,
        out_shape=jax.ShapeDtypeStruct((m, d), F32),
        compiler_params=_params(1),
        name="norm_glu",
    )(h, g, w)


def _qkv_kernel(h_ref, g_ref, w_ref, gq_ref, gk_ref, hm_ref, q_ref, k_ref, v_ref, *, hd):
    d = h_ref.shape[1]
    n = _rms(h_ref[...], g_ref[...]).astype(BF16)
    hm = hm_ref[...]

    def head_norm(x, gain):
        x2 = (x * x).astype(BF16)
        ms = jnp.concatenate(
            [_dot(x2[:, c * LANES:(c + 1) * LANES], hm) for c in range(d // LANES)], axis=1)
        return x * lax.rsqrt(ms * (1.0 / hd) + EPS) * gain

    q = head_norm(_dot(n, w_ref[:, :d]), gq_ref[...])
    q_ref[...] = q * (hd ** -0.5)
    k_ref[...] = head_norm(_dot(n, w_ref[:, d:2 * d]), gk_ref[...])
    v_ref[...] = _dot(n, w_ref[:, 2 * d:])


def _qkv(h, g, w, layer, gq, gk, hm, hd):
    m, d = h.shape
    tm = _row_tile(m)
    row = pl.BlockSpec((tm, d), lambda i: (i, 0))
    out = jax.ShapeDtypeStruct((m, d), F32)
    return pl.pallas_call(
        functools.partial(_qkv_kernel, hd=hd),
        grid=(m // tm,),
        in_specs=[row, _const_spec((1, d)), _layer_spec(w, layer), _const_spec((1, d)),
                  _const_spec((1, d)), _const_spec(hm.shape)],
        out_specs=[row, row, row],
        out_shape=[out, out, out],
        compiler_params=_params(1),
        name="qkv",
    )(h, g, w, gq, gk, hm)


def _softplus(z):
    return jnp.maximum(z, 0.0) + jnp.log(1.0 + jnp.exp(-jnp.abs(z)))


def _hi_lo(x):
    hi = x.astype(BF16)
    lo = (x - hi.astype(F32)).astype(BF16)
    return jnp.concatenate([hi, lo], axis=1)


def _suffix_matrix(n, parts):
    i = jnp.arange(n)
    tri = i[:, None] >= i[None, :]
    return jnp.concatenate([tri] * parts, axis=0).astype(BF16)


def _attn_prompt_kernel(bias_ref, q_ref, k_ref, v_ref, tri_ref, o_ref, k16, vm16, carry_ref,
                        acc_ref, z_ref, sp_ref, *, hd, tk):
    hg = pl.program_id(1)
    qi = pl.program_id(2)
    tq = q_ref.shape[0]
    t_pad = k_ref.shape[0]
    nsub = tq // tk
    hpl = LANES // hd
    nlb = q_ref.shape[1] // LANES
    nh = nlb * hpl
    lane = lax.broadcasted_iota(jnp.int32, (tq, LANES), 1)

    @pl.when(qi == 0)
    def _():
        k16[0:t_pad, :] = k_ref[...].astype(BF16)
        v = v_ref[...]
        lanes = lax.broadcasted_iota(jnp.int32, v.shape, 1) % LANES
        tail = k16.shape[0] - t_pad
        for j in range(hpl):
            vm16[j, 0:t_pad, :] = jnp.where(lanes // hd == j, v, 0.0).astype(BF16)
            if tail:
                vm16[j, t_pad:, :] = jnp.zeros((tail, k16.shape[1]), BF16)
        if tail:
            k16[t_pad:, :] = jnp.zeros((tail, k16.shape[1]), BF16)

    tri = tri_ref[...]
    qm = []
    for p in range(nlb):
        qp = q_ref[:, p * LANES:(p + 1) * LANES]
        qm.append(jnp.concatenate([jnp.where(lane // hd == j, qp, 0.0) for j in range(hpl)],
                                  axis=0).astype(BF16))
    bias = [[bias_ref[(hg * nlb + p) * hpl + j] for j in range(hpl)] for p in range(nlb)]
    carry_ref[...] = jnp.zeros(carry_ref.shape, F32)
    acc_ref[...] = jnp.zeros(acc_ref.shape, F32)

    def diagonal_tile(s):
        k0 = pl.multiple_of((qi * nsub + s) * tk, tk)
        lo = s * tk
        mr = tq - lo
        qpos = qi * tq + lo + lax.broadcasted_iota(jnp.int32, (mr, tk), 0)
        mask = k0 + lax.broadcasted_iota(jnp.int32, (mr, tk), 1) < qpos
        zs, sps = [], []
        for p in range(nlb):
            qrows = jnp.concatenate([qm[p][j * tq + lo:(j + 1) * tq] for j in range(hpl)], axis=0)
            zp = _dot_t(qrows, k16[pl.ds(k0, tk), p * LANES:(p + 1) * LANES])
            for j in range(hpl):
                z = zp[j * mr:(j + 1) * mr] + bias[p][j]
                zs.append(z)
                sps.append(jnp.where(mask, _softplus(z), 0.0))
        suffix = _dot(jnp.concatenate([sp.astype(BF16) for sp in sps], axis=0), tri)
        for p in range(nlb):
            ws = []
            for j in range(hpl):
                i = p * hpl + j
                carry = carry_ref[i, lo:, :]
                w = jnp.exp(zs[i] - suffix[i * mr:(i + 1) * mr] - carry)
                ws.append(jnp.where(mask, w, 0.0).astype(BF16))
                carry_ref[i, lo:, :] = carry + jnp.sum(sps[i], axis=1, keepdims=True)
            vm = jnp.concatenate(
                [vm16[j, pl.ds(k0, tk), p * LANES:(p + 1) * LANES] for j in range(hpl)], axis=0)
            acc_ref[p, lo:, :] += _dot(jnp.concatenate(ws, axis=1), vm)

    for s in reversed(range(nsub)):
        diagonal_tile(s)

    def logits(kt, slot):
        k0 = pl.multiple_of(kt * tk, tk)
        for p in range(nlb):
            zp = _dot_t(qm[p], k16[pl.ds(k0, tk), p * LANES:(p + 1) * LANES])
            for j in range(hpl):
                z = zp[j * tq:(j + 1) * tq] + bias[p][j]
                z_ref[slot, p * hpl + j] = z
                sp_ref[slot, p * hpl + j] = _softplus(z)

    def weigh(kt, slot):
        k0 = pl.multiple_of(kt * tk, tk)
        sps = [sp_ref[slot, i] for i in range(nh)]
        suffix = _dot(jnp.concatenate([sp.astype(BF16) for sp in sps], axis=0), tri)
        for p in range(nlb):
            ws = []
            for j in range(hpl):
                i = p * hpl + j
                carry = carry_ref[i]
                w = jnp.exp(z_ref[slot, i] - suffix[i * tq:(i + 1) * tq] - carry)
                ws.append(w.astype(BF16))
                carry_ref[i] = carry + jnp.sum(sps[i], axis=1, keepdims=True)
            vm = jnp.concatenate(
                [vm16[j, pl.ds(k0, tk), p * LANES:(p + 1) * LANES] for j in range(hpl)], axis=0)
            acc_ref[p] += _dot(jnp.concatenate(ws, axis=1), vm)

    n = qi * nsub

    @pl.when(n > 0)
    def _():
        logits(n - 1, 0)

    def body(i, c):
        kt = n - 1 - i
        weigh(kt, i % 2)
        logits(kt - 1, (i + 1) % 2)
        return c

    lax.fori_loop(0, n - 1, body, 0)

    @pl.when(n > 0)
    def _():
        weigh(0, (n - 1) % 2)

    o_ref[...] = jnp.concatenate([acc_ref[p] for p in range(nlb)], axis=1)


def _attn_prompt(q, k, v, bias, tri, hd):
    nb, t_pad, d = q.shape
    tk = SEQ_TILE
    tq = Q_TILE
    nqt = pl.cdiv(t_pad, tq)
    hpl = LANES // hd
    wg = min(d, 4 * LANES)
    assert d % wg == 0 and tq % tk == 0 and t_pad % SEQ_PAD == 0
    seq = pl.BlockSpec((None, t_pad, wg), lambda b, hg, qi: (b, 0, hg))
    tile = pl.BlockSpec((None, tq, wg), lambda b, hg, qi: (b, qi, hg))
    return pl.pallas_call(
        functools.partial(_attn_prompt_kernel, hd=hd, tk=tk),
        grid=(nb, d // wg, nqt),
        in_specs=[pl.BlockSpec(memory_space=pltpu.SMEM), tile, seq, seq, _const_spec(tri.shape)],
        out_specs=tile,
        out_shape=jax.ShapeDtypeStruct((nb, t_pad, d), F32),
        scratch_shapes=[pltpu.VMEM((nqt * tq, wg), BF16), pltpu.VMEM((hpl, nqt * tq, wg), BF16),
                        pltpu.VMEM((wg // hd, tq, tk), F32),
                        pltpu.VMEM((wg // LANES, tq, LANES), F32),
                        pltpu.VMEM((2, wg // hd, tq, tk), F32),
                        pltpu.VMEM((2, wg // hd, tq, tk), F32)],
        compiler_params=_params(3),
        name="attn_prompt",
    )(bias, q, k, v, tri)


def _attn_sample_kernel(pt_ref, q_ref, kn_ref, vn_ref, *refs, nh, hd, pps, n_steps):
    del pt_ref
    kc_refs, vc_refs = refs[:pps], refs[pps:2 * pps]
    bias_ref, tri_ref, o_ref, qbd_ref, carry_ref, acc_ref = refs[2 * pps:]
    s = pl.program_id(1)
    tq, d = q_ref.shape
    tk = kc_refs[0].shape[1]
    r = nh * tq
    hpl = LANES // hd
    rp = hpl * tq
    nlb = d // LANES
    kc = min(d, 2 * LANES)
    rk = kc // hd * tq
    tri = tri_ref[...]
    bias = bias_ref[...]

    @pl.when(s == 0)
    def _():
        row_h = lax.broadcasted_iota(jnp.int32, (r, d), 0) // tq
        lane_h = lax.broadcasted_iota(jnp.int32, (r, d), 1) // hd
        qbd = jnp.where(row_h == lane_h, jnp.tile(q_ref[...], (nh, 1)), 0.0).astype(BF16)
        qbd_ref[...] = qbd
        pad = jnp.zeros((tk - tq, d), F32)
        kb = jnp.concatenate([kn_ref[...], pad], axis=0).astype(BF16)
        vb = jnp.concatenate([vn_ref[...], pad], axis=0).astype(BF16)
        qpos = lax.broadcasted_iota(jnp.int32, (r, tk), 0) % tq
        kpos = lax.broadcasted_iota(jnp.int32, (r, tk), 1)
        mask = kpos < qpos
        z = _dot_t(qbd, kb) + bias
        sp = jnp.where(mask, _softplus(z), 0.0)
        w = jnp.where(mask, jnp.exp(z - _dot(_hi_lo(sp), tri)), 0.0).astype(BF16)
        carry_ref[...] = jnp.zeros((r, tk), F32) + jnp.sum(sp, axis=1, keepdims=True)
        pv = _dot(w, vb)
        acc_ref[...] = jnp.concatenate(
            [pv[c * rp:(c + 1) * rp, c * LANES:(c + 1) * LANES] for c in range(nlb)], axis=0)

    kcat = jnp.concatenate([kc_refs[i][...].astype(BF16) for i in range(pps)], axis=1)
    zcat = jnp.concatenate(
        [_dot(qbd_ref[c * rk:(c + 1) * rk, c * kc:(c + 1) * kc], kcat[c * kc:(c + 1) * kc, :])
         for c in range(d // kc)], axis=0)
    zs = [zcat[:, i * tk:(i + 1) * tk] + bias for i in range(pps)]
    sps = [_softplus(z) for z in zs]
    suffix = _dot(jnp.concatenate([_hi_lo(sp) for sp in sps], axis=0), tri)
    carry = carry_ref[...]
    ws = []
    for i in range(pps):
        ws.append(jnp.exp(zs[i] - suffix[i * r:(i + 1) * r] - carry).astype(BF16))
        carry = carry + jnp.sum(sps[i], axis=1, keepdims=True)
    carry_ref[...] = carry
    wcat = jnp.concatenate(ws, axis=1)
    vcat = jnp.concatenate([vc_refs[i][...].astype(BF16) for i in range(pps)], axis=1)
    acc_ref[...] += jnp.concatenate(
        [_dot_t(wcat[c * rp:(c + 1) * rp, :], vcat[c * LANES:(c + 1) * LANES, :])
         for c in range(nlb)], axis=0)

    @pl.when(s == n_steps - 1)
    def _():
        lane_h = lax.broadcasted_iota(jnp.int32, (tq, LANES), 1) // hd
        cols = []
        for c in range(nlb):
            o = acc_ref[c * rp:c * rp + tq, :]
            for j in range(1, hpl):
                o = jnp.where(lane_h == j, acc_ref[c * rp + j * tq:c * rp + (j + 1) * tq, :], o)
            cols.append(o)
        o_ref[...] = jnp.concatenate(cols, axis=1)


def _attn_sample(q, k, v, cache_kt, cache_vt, layer, page_table, bias_rows, tri, tq, nh, hd):
    m, d = q.shape
    nb, n_pages = page_table.shape
    tk = cache_kt.shape[3]
    r = nh * tq
    pps = max(p for p in range(1, 9) if n_pages % p == 0)
    n_steps = n_pages // pps
    new_rows = pl.BlockSpec((tq, d), lambda b, s, pt: (b, 0))

    def page_spec(i):
        def page_map(b, s, pt):
            return (layer, pt[b * n_pages + n_pages - 1 - (s * pps + i)], 0, 0)
        return pl.BlockSpec((None, None, d, tk), page_map)

    pages = [page_spec(i) for i in range(pps)]
    grid_spec = pltpu.PrefetchScalarGridSpec(
        num_scalar_prefetch=1,
        grid=(nb, n_steps),
        in_specs=[new_rows, new_rows, new_rows] + pages + pages
        + [_const_spec(bias_rows.shape), _const_spec(tri.shape)],
        out_specs=new_rows,
        scratch_shapes=[pltpu.VMEM((r, d), BF16), pltpu.VMEM((r, tk), F32),
                        pltpu.VMEM((r, LANES), F32)],
    )
    return pl.pallas_call(
        functools.partial(_attn_sample_kernel, nh=nh, hd=hd, pps=pps, n_steps=n_steps),
        grid_spec=grid_spec,
        out_shape=jax.ShapeDtypeStruct((m, d), F32),
        compiler_params=_params(2),
        name="attn_sample",
    )(page_table.reshape(-1), q, k, v, *([cache_kt] * pps), *([cache_vt] * pps), bias_rows, tri)


def _conv_prompt_kernel(halo_ref, u_ref, h_ref, wdw_ref, gn_ref, w2_ref, o_ref, ext_ref, y_ref,
                        *, nb, rc):
    i = pl.program_id(0)
    rows = u_ref.shape[0]
    hrows = halo_ref.shape[0]
    width = wdw_ref.shape[0]
    ext_ref[0:hrows, :] = jnp.where(i == 0, 0.0, halo_ref[...])
    ext_ref[hrows:, :] = u_ref[...]

    def chunk(c, carry):
        r0 = pl.multiple_of(c * rc, rc)
        y = jnp.zeros((rc, u_ref.shape[1]), F32)
        for j in range(width):
            start = hrows - (width - 1 - j) * nb
            x = ext_ref[pl.ds(pl.multiple_of(r0 + start, SUBLANES), rc), :]
            y = y + x * wdw_ref[j:j + 1, :]
        y_ref[pl.ds(r0, rc), :] = y
        return carry

    lax.fori_loop(0, rows // rc, chunk, 0)
    yn = _rms(y_ref[...], gn_ref[...])
    o_ref[...] = h_ref[...] + _dot((yn * _sigmoid(yn)).astype(BF16), w2_ref[...])


def _conv_prompt(h, u, wdw, gn, w2, layer, nb):
    m, d = h.shape
    assert nb % SUBLANES == 0, "a time shift must move whole sublane groups"
    need = (wdw.shape[0] - 1) * nb
    hrows = next((r for r in range(_round_up(need, LANES), ROW_TILE + 1, LANES) if m % r == 0),
                 need)
    while m % hrows:
        hrows += SUBLANES
    per = max(k for k in range(1, max(ROW_TILE // hrows, 1) + 1) if m % (k * hrows) == 0)
    tm = per * hrows
    rc = _largest_tile(tm, 4 * SUBLANES)
    tile = pl.BlockSpec((tm, d), lambda i: (i, 0))
    halo = pl.BlockSpec((hrows, d), lambda i: (jnp.maximum(i * per - 1, 0), 0))
    return pl.pallas_call(
        functools.partial(_conv_prompt_kernel, nb=nb, rc=rc),
        grid=(m // tm,),
        in_specs=[halo, tile, tile, _const_spec(wdw.shape), _const_spec((1, d)),
                  _layer_spec(w2, layer)],
        out_specs=tile,
        out_shape=jax.ShapeDtypeStruct((m, d), F32),
        input_output_aliases={2: 0},
        scratch_shapes=[pltpu.VMEM((hrows + tm, d), F32), pltpu.VMEM((tm, d), F32)],
        compiler_params=_params(1),
        name="conv_prompt",
    )(u, u, h, wdw, gn, w2)


def _conv_sample_kernel(st_ref, u_ref, h_ref, wdw_ref, gn_ref, w2_ref, o_ref):
    width = wdw_ref.shape[0]
    ns = st_ref.shape[0]
    for t in range(u_ref.shape[0]):
        y = jnp.zeros(h_ref.shape[1:], F32)
        for j in range(width):
            i = t + j
            x = st_ref[i] if i < ns else u_ref[i - ns]
            y = y + x * wdw_ref[j:j + 1, :]
        yn = _rms(y, gn_ref[...])
        o_ref[t] = h_ref[t] + _dot((yn * _sigmoid(yn)).astype(BF16), w2_ref[...])


def _conv_sample(h_t, u_t, st_t, wdw, gn, w2, layer):
    t, nb, d = h_t.shape
    bs = _largest_tile(nb, 32)
    blk = lambda n: pl.BlockSpec((n, bs, d), lambda i: (0, i, 0))
    return pl.pallas_call(
        _conv_sample_kernel,
        grid=(nb // bs,),
        in_specs=[blk(st_t.shape[0]), blk(t), blk(t), _const_spec(wdw.shape),
                  _const_spec((1, d)), _layer_spec(w2, layer)],
        out_specs=blk(t),
        out_shape=jax.ShapeDtypeStruct((t, nb, d), F32),
        compiler_params=_params(1),
        name="conv_sample",
    )(st_t, u_t, h_t, wdw, gn, w2)


def _ssm_prep_kernel(are_ref, aim_ref, ldt_ref, bre_ref, bim_ref,
                     pwre_ref, pwim_ref, bbre_ref, bbim_ref):
    a_re, a_im = are_ref[...], aim_ref[...]
    dt = jnp.exp(ldt_ref[...])
    mag = jnp.exp(dt * a_re)
    ab_re, ab_im = mag * jnp.cos(dt * a_im), mag * jnp.sin(dt * a_im)
    den = a_re * a_re + a_im * a_im
    nr, ni = ab_re - 1.0, ab_im
    f_re = (nr * a_re + ni * a_im) / den
    f_im = (ni * a_re - nr * a_im) / den
    b_re, b_im = bre_ref[...], bim_ref[...]
    bbre_ref[...] = f_re * b_re - f_im * b_im
    bbim_ref[...] = f_re * b_im + f_im * b_re
    p_re, p_im = ab_re, ab_im
    for k in range(pwre_ref.shape[0]):
        pwre_ref[k:k + 1, :] = p_re
        pwim_ref[k:k + 1, :] = p_im
        p_re, p_im = p_re * ab_re - p_im * ab_im, p_re * ab_im + p_im * ab_re


def _ssm_prep(a_re, a_im, ldt, b_re, b_im):
    gp = a_re.shape[1]
    gc = b_re.shape[0]
    vec = jax.ShapeDtypeStruct((SUBLANES, gp), F32)
    mat = jax.ShapeDtypeStruct((gc, gp), F32)
    return pl.pallas_call(
        _ssm_prep_kernel,
        out_shape=[vec, vec, mat, mat],
        compiler_params=pltpu.CompilerParams(vmem_limit_bytes=VMEM_LIMIT),
        name="ssm_prep",
    )(a_re, a_im, ldt, b_re, b_im)


def _ssm_kernel(*refs, nb, t_last, lc):
    chained = nb > 0
    if chained:
        (h_ref, g_ref, bbre_ref, bbim_ref, ccre_ref, ccim_ref, pwre_ref, pwim_ref, dsk_ref,
         wglu_ref, o_ref, stre_ref, stim_ref, sre, sim, cre, cim) = refs
    else:
        (h_ref, g_ref, bbre_ref, bbim_ref, ccre_ref, ccim_ref, pwre_ref, pwim_ref, dsk_ref,
         wglu_ref, h0re_ref, h0im_ref, o_ref, stre_ref, stim_ref, sre, sim) = refs
    rows, d = h_ref.shape
    gp = sre.shape[1]
    nblk = d // LANES
    cw = gp // nblk

    h = h_ref[...]
    n = _rms(h, g_ref[...])
    nb16 = n.astype(BF16)
    for c in range(nblk):
        x = nb16[:, c * LANES:(c + 1) * LANES]
        sre[:, c * cw:(c + 1) * cw] = _dot(x, bbre_ref[c])
        sim[:, c * cw:(c + 1) * cw] = _dot(x, bbim_ref[c])

    if chained:
        step = pl.program_id(0)

        @pl.when(step == 0)
        def _():
            cre[...] = jnp.zeros(cre.shape, F32)
            cim[...] = jnp.zeros(cim.shape, F32)

        for c in range(gp // lc):
            ls = slice(c * lc, (c + 1) * lc)
            a_re = jnp.broadcast_to(pwre_ref[0:1, ls], (nb, lc))
            a_im = jnp.broadcast_to(pwim_ref[0:1, ls], (nb, lc))

            def body(t, carry, ls=ls, a_re=a_re, a_im=a_im):
                x_re, x_im = carry
                r0 = pl.multiple_of(t * nb, nb)
                x_re, x_im = (a_re * x_re - a_im * x_im + sre[pl.ds(r0, nb), ls],
                              a_re * x_im + a_im * x_re + sim[pl.ds(r0, nb), ls])
                sre[pl.ds(r0, nb), ls] = x_re
                sim[pl.ds(r0, nb), ls] = x_im
                return x_re, x_im

            x_re, x_im = lax.fori_loop(0, rows // nb, body, (cre[:, ls], cim[:, ls]), unroll=8)
            cre[:, ls] = x_re
            cim[:, ls] = x_im

        @pl.when(step == t_last * nb // rows)
        def _():
            r0 = t_last * nb % rows
            stre_ref[...] = sre[r0:r0 + nb, :]
            stim_ref[...] = sim[r0:r0 + nb, :]
    else:
        rowi = lax.broadcasted_iota(jnp.int32, (SUBLANES, lc), 0)
        for c in range(gp // lc):
            ls = slice(c * lc, (c + 1) * lc)
            p_re, p_im = pwre_ref[:, ls], pwim_ref[:, ls]

            def body(r, carry, ls=ls, p_re=p_re, p_im=p_im):
                r0 = pl.multiple_of(r * SUBLANES, SUBLANES)
                c_re, c_im = h0re_ref[pl.ds(r, 1), ls], h0im_ref[pl.ds(r, 1), ls]
                x_re, x_im = sre[pl.ds(r0, SUBLANES), ls], sim[pl.ds(r0, SUBLANES), ls]
                for dist in (1, 2, 4):
                    a_re, a_im = p_re[dist - 1:dist], p_im[dist - 1:dist]
                    s_re = jnp.where(rowi >= dist, pltpu.roll(x_re, dist, axis=0), 0.0)
                    s_im = jnp.where(rowi >= dist, pltpu.roll(x_im, dist, axis=0), 0.0)
                    x_re, x_im = (x_re + a_re * s_re - a_im * s_im,
                                  x_im + a_re * s_im + a_im * s_re)
                x_re, x_im = (x_re + p_re * c_re - p_im * c_im,
                              x_im + p_re * c_im + p_im * c_re)
                sre[pl.ds(r0, SUBLANES), ls] = x_re
                sim[pl.ds(r0, SUBLANES), ls] = x_im
                stre_ref[pl.ds(r, 1), ls] = x_re[SUBLANES - 1:]
                stim_ref[pl.ds(r, 1), ls] = x_im[SUBLANES - 1:]
                return carry

            lax.fori_loop(0, rows // SUBLANES, body, 0)

    hre = sre[...].astype(BF16)
    him = sim[...].astype(BF16)
    y = jnp.concatenate(
        [_dot(hre[:, c * cw:(c + 1) * cw], ccre_ref[c]) - _dot(him[:, c * cw:(c + 1) * cw], ccim_ref[c])
         for c in range(nblk)], axis=1)
    y = (y + dsk_ref[...] * n).astype(BF16)
    za = _dot(y, wglu_ref[:, :d])
    zb = _dot(y, wglu_ref[:, d:])
    o_ref[...] = h + za * _sigmoid(zb)


def _ssm_common_specs(d, bb, cc, pw, wglu, layer):
    return [_const_spec((1, d)), _const_spec(bb.shape), _const_spec(bb.shape),
            _const_spec(cc.shape), _const_spec(cc.shape), _const_spec(pw.shape),
            _const_spec(pw.shape), _const_spec((1, d)), _layer_spec(wglu, layer)]


def _ssm_prompt(h, g, bbre, bbim, ccre, ccim, pwre, pwim, dsk, wglu, layer, nb, t_valid):
    m, d = h.shape
    gp = pwre.shape[1]
    assert nb % SUBLANES == 0, "one time step of all sequences must fill whole sublane groups"
    rows = _largest_tile(m, 256, nb)
    lc = min(SSM_LANE_CHUNK, gp)
    tile = pl.BlockSpec((rows, d), lambda i: (i, 0))
    st = _const_spec((nb, gp))
    st_shape = jax.ShapeDtypeStruct((nb, gp), F32)
    return pl.pallas_call(
        functools.partial(_ssm_kernel, nb=nb, t_last=t_valid - 1, lc=lc),
        grid=(m // rows,),
        in_specs=[tile] + _ssm_common_specs(d, bbre, ccre, pwre, wglu, layer),
        out_specs=[tile, pl.BlockSpec((nb, gp), lambda i: (0, 0)),
                   pl.BlockSpec((nb, gp), lambda i: (0, 0))],
        out_shape=[jax.ShapeDtypeStruct((m, d), F32), st_shape, st_shape],
        input_output_aliases={0: 0},
        scratch_shapes=[pltpu.VMEM((rows, gp), F32), pltpu.VMEM((rows, gp), F32),
                        pltpu.VMEM((nb, gp), F32), pltpu.VMEM((nb, gp), F32)],
        compiler_params=_params(1),
        name="ssm_prompt",
    )(h, g, bbre, bbim, ccre, ccim, pwre, pwim, dsk, wglu)


def _ssm_sample(h, g, bbre, bbim, ccre, ccim, pwre, pwim, dsk, wglu, layer, h0re, h0im, t):
    m, d = h.shape
    assert t == SUBLANES, "the sample scan handles one 8-token block per sequence"
    nb, gp = h0re.shape
    bs = _largest_tile(nb, 32)
    rows = bs * t
    lc = min(SSM_LANE_CHUNK, gp)
    tile = pl.BlockSpec((rows, d), lambda i: (i, 0))
    st = pl.BlockSpec((bs, gp), lambda i: (i, 0))
    st_shape = jax.ShapeDtypeStruct((nb, gp), F32)
    return pl.pallas_call(
        functools.partial(_ssm_kernel, nb=0, t_last=0, lc=lc),
        grid=(nb // bs,),
        in_specs=[tile] + _ssm_common_specs(d, bbre, ccre, pwre, wglu, layer) + [st, st],
        out_specs=[tile, st, st],
        out_shape=[jax.ShapeDtypeStruct((m, d), F32), st_shape, st_shape],
        input_output_aliases={0: 0},
        scratch_shapes=[pltpu.VMEM((rows, gp), F32), pltpu.VMEM((rows, gp), F32)],
        compiler_params=_params(1),
        name="ssm_sample",
    )(h, g, bbre, bbim, ccre, ccim, pwre, pwim, dsk, wglu, h0re, h0im)


def _block_diag_blocks(full, row_group, col_group, row_blk, col_blk):
    rows, cols = full.shape
    keep = (jnp.arange(rows)[:, None] // row_group) == (jnp.arange(cols)[None, :] // col_group)
    full = jnp.where(keep, full, 0.0)
    return jnp.stack([full[i * row_blk:(i + 1) * row_blk, i * col_blk:(i + 1) * col_blk]
                      for i in range(rows // row_blk)]).astype(BF16)


def kernel(x_prompt, x_sample, cache_k, cache_v, page_table, state_conv, state_ssm_re, state_ssm_im, meta_tokens, norm_g, ffn1_w_gate, ffn1_w_up, ffn1_w_down, ffn2_w_gate, ffn2_w_up, ffn2_w_down, sb_w_qkv, sb_w_o, sb_q_norm, sb_k_norm, sb_logit_bias, conv_w_pw1, conv_w_dw, conv_norm_g, conv_w_pw2, ssm_A_re, ssm_A_im, ssm_log_dt, ssm_B_re, ssm_B_im, ssm_C_re, ssm_C_im, ssm_D, ssm_w_glu):
    nb, seq, d = x_prompt.shape
    nsb, ts_s, _ = x_sample.shape
    depth = norm_g.shape[0]
    n_meta = meta_tokens.shape[0]
    nh = sb_logit_bias.shape[1]
    hd = d // nh
    t_valid = seq + n_meta
    t_pad = _round_up(t_valid, SEQ_PAD)
    m_p = nb * t_pad
    m_s = nsb * ts_s
    assert d % LANES == 0 and LANES % hd == 0

    meta = jnp.broadcast_to(meta_tokens[None], (nb, n_meta, d))
    pad = jnp.zeros((nb, t_pad - t_valid, d), F32)
    hp = jnp.concatenate([meta, x_prompt, pad], axis=1).reshape(m_p, d)
    hs = x_sample.reshape(m_s, d)
    time_major = False

    def set_order(x, want_time_major):
        if want_time_major == time_major:
            return x
        a, b = (nb, t_pad) if want_time_major else (t_pad, nb)
        return x.reshape(a, b, d).transpose(1, 0, 2).reshape(m_p, d)

    page = cache_k.shape[2]
    cache_kt = cache_k.transpose(0, 1, 3, 4, 2).reshape(cache_k.shape[:2] + (d, page))
    cache_vt = cache_v.transpose(0, 1, 3, 4, 2).reshape(cache_v.shape[:2] + (d, page))
    idx = jnp.arange(LANES)
    head_ones = (idx[:, None] // hd == idx[None, :] // hd).astype(BF16)

    ffn1 = [w.astype(BF16) for w in (ffn1_w_gate, ffn1_w_up, ffn1_w_down)]
    ffn2 = [w.astype(BF16) for w in (ffn2_w_gate, ffn2_w_up, ffn2_w_down)]
    w_qkv, w_o = sb_w_qkv.astype(BF16), sb_w_o.astype(BF16)
    w_pw1, w_pw2 = conv_w_pw1.astype(BF16), conv_w_pw2.astype(BF16)
    w_glu = ssm_w_glu.astype(BF16)

    new_k, new_v, new_conv, new_re, new_im = [], [], [], [], []
    for i in range(depth):
        kind, j = i % 3, i // 3
        g1, g_mix, g2 = norm_g[i, 0][None], norm_g[i, 1][None], norm_g[i, 2][None]
        hp, hs = _ffn(hp, g1, *ffn1, i), _ffn(hs, g1, *ffn1, i)
        if kind == 0:
            hp = set_order(hp, False)
            time_major = False
            gq, gk = jnp.tile(sb_q_norm[j], nh)[None], jnp.tile(sb_k_norm[j], nh)[None]
            qp, kp, vp = [x.reshape(nb, t_pad, d)
                          for x in _qkv(hp, g_mix, w_qkv, j, gq, gk, head_ones, hd)]
            qs, ks, vs = _qkv(hs, g_mix, w_qkv, j, gq, gk, head_ones, hd)
            bias = sb_logit_bias[j]
            op = _attn_prompt(qp, kp, vp, bias, _suffix_matrix(SEQ_TILE, 1), hd)
            bias_rows = jnp.broadcast_to(jnp.repeat(bias, ts_s)[:, None], (nh * ts_s, page))
            os_ = _attn_sample(qs, ks, vs, cache_kt, cache_vt, j, page_table, bias_rows,
                               _suffix_matrix(page, 2), ts_s, nh, hd)
            hp = _proj_res(hp, op.reshape(m_p, d), w_o, j)
            hs = _proj_res(hs, os_, w_o, j)
            new_k.append((kp[:, :t_valid].reshape(nb, t_valid, nh, hd),
                          ks.reshape(nsb, ts_s, nh, hd)))
            new_v.append((vp[:, :t_valid].reshape(nb, t_valid, nh, hd),
                          vs.reshape(nsb, ts_s, nh, hd)))
        elif kind == 1:
            hp = set_order(hp, True)
            time_major = True
            wdw, gn = conv_w_dw[j], conv_norm_g[j][None]
            width = wdw.shape[0]
            up = _norm_glu(hp, g_mix, w_pw1, j)
            hp = _conv_prompt(hp, up, wdw, gn, w_pw2, j, nb)
            us = _norm_glu(hs, g_mix, w_pw1, j).reshape(nsb, ts_s, d)
            hs_t = _conv_sample(hs.reshape(nsb, ts_s, d).transpose(1, 0, 2), us.transpose(1, 0, 2),
                                state_conv[j].transpose(1, 0, 2), wdw, gn, w_pw2, j)
            hs = hs_t.transpose(1, 0, 2).reshape(m_s, d)
            up_tail = up.reshape(t_pad, nb, d)[t_valid - (width - 1):t_valid]
            new_conv.append((up_tail.transpose(1, 0, 2),
                             jnp.concatenate([state_conv[j], us], axis=1)[:, -(width - 1):]))
        else:
            hp = set_order(hp, True)
            time_major = True
            ng, ns = ssm_A_re.shape[1:]
            gc = ssm_B_re.shape[3]
            gp = ng * ns
            pwre, pwim, bbre, bbim = _ssm_prep(
                ssm_A_re[j].reshape(1, gp), ssm_A_im[j].reshape(1, gp),
                jnp.repeat(ssm_log_dt[j], ns)[None],
                ssm_B_re[j].transpose(2, 0, 1).reshape(gc, gp),
                ssm_B_im[j].transpose(2, 0, 1).reshape(gc, gp))
            gpb = LANES // gc
            bb = [_block_diag_blocks(jnp.tile(x, (ng, 1)), gc, ns, LANES, gpb * ns)
                  for x in (bbre, bbim)]
            cc = [_block_diag_blocks(jnp.tile(x[j].transpose(0, 2, 1).reshape(gp, gc), (1, ng)),
                                     ns, gc, gpb * ns, LANES) for x in (ssm_C_re, ssm_C_im)]
            args = (g_mix, bb[0], bb[1], cc[0], cc[1], pwre, pwim, ssm_D[j][None], w_glu, j)
            hp, pre, pim = _ssm_prompt(hp, *args, nb, t_valid)
            hs, sre, sim = _ssm_sample(hs, *args, state_ssm_re[j].reshape(nsb, gp),
                                       state_ssm_im[j].reshape(nsb, gp), ts_s)
            new_re.append((pre.reshape(nb, ng, ns), sre.reshape(nsb, ng, ns)))
            new_im.append((pim.reshape(nb, ng, ns), sim.reshape(nsb, ng, ns)))
        hp, hs = _ffn(hp, g2, *ffn2, i), _ffn(hs, g2, *ffn2, i)

    def stack(pairs, which):
        return jnp.stack([p[which] for p in pairs])

    hp = set_order(hp, False)
    y_prompt = hp.reshape(nb, t_pad, d)[:, n_meta:t_valid]
    y_sample = hs.reshape(nsb, ts_s, d)
    return (y_prompt, y_sample, stack(new_k, 0), stack(new_v, 0), stack(new_k, 1),
            stack(new_v, 1), stack(new_conv, 0), stack(new_conv, 1), stack(new_re, 0),
            stack(new_im, 0), stack(new_re, 1), stack(new_im, 1))
```

```python
import functools

import jax
import jax.numpy as jnp
from jax import lax
from jax.experimental import pallas as pl
from jax.experimental.pallas import tpu as pltpu

F32 = jnp.float32
BF16 = jnp.bfloat16
EPS = 1e-6
LANES = 128
SUBLANES = 8
VMEM_LIMIT = 56 * 1024 * 1024
SEQ_TILE = 128
SEQ_PAD = 128
Q_TILE = 256
ROW_TILE = 512
SSM_LANE_CHUNK = 512


def _round_up(x, m):
    return (x + m - 1) // m * m


def _largest_tile(n, cap, mult=SUBLANES):
    best = mult
    for t in range(mult, cap + 1, mult):
        if n % t == 0:
            best = t
    return best


def _row_tile(m):
    tm = _largest_tile(m, ROW_TILE, LANES)
    return tm if m % tm == 0 else _largest_tile(m, ROW_TILE)


def _params(n_grid):
    return pltpu.CompilerParams(dimension_semantics=("arbitrary",) * n_grid,
                                vmem_limit_bytes=VMEM_LIMIT)


def _const_spec(shape):
    nd = len(shape)
    return pl.BlockSpec(shape, lambda *_: (0,) * nd, pipeline_mode=pl.Buffered(1))


def _layer_spec(w, layer):
    nd = w.ndim - 1
    return pl.BlockSpec((None,) + w.shape[1:], lambda *_: (layer,) + (0,) * nd,
                        pipeline_mode=pl.Buffered(1))


def _dot(a, b):
    return jnp.dot(a, b, preferred_element_type=F32)


def _dot_t(a, b):
    return lax.dot_general(a, b, (((1,), (1,)), ((), ())), preferred_element_type=F32)


def _rms(x, g):
    ms = jnp.mean(x * x, axis=-1, keepdims=True)
    return x * lax.rsqrt(ms + EPS) * g


def _sigmoid(x):
    return 1.0 / (1.0 + jnp.exp(-x))


def _ffn_kernel(h_ref, g_ref, wg_ref, wu_ref, wd_ref, o_ref, *, fc):
    h = h_ref[...]
    n = _rms(h, g_ref[...]).astype(BF16)
    acc = jnp.zeros(h.shape, F32)
    for c in range(wg_ref.shape[1] // fc):
        sl = slice(c * fc, (c + 1) * fc)
        gate = _dot(n, wg_ref[:, sl])
        up = _dot(n, wu_ref[:, sl])
        a = (gate * _sigmoid(gate) * up).astype(BF16)
        acc = acc + _dot(a, wd_ref[sl, :])
    o_ref[...] = h + 0.5 * acc


def _ffn(h, g, wg, wu, wd, layer):
    m, d = h.shape
    tm = _row_tile(m)
    f = wg.shape[2]
    fc = 256 if f % 256 == 0 else LANES
    row = pl.BlockSpec((tm, d), lambda i: (i, 0))
    return pl.pallas_call(
        functools.partial(_ffn_kernel, fc=fc),
        grid=(m // tm,),
        in_specs=[row, _const_spec((1, d)), _layer_spec(wg, layer), _layer_spec(wu, layer),
                  _layer_spec(wd, layer)],
        out_specs=row,
        out_shape=jax.ShapeDtypeStruct((m, d), F32),
        input_output_aliases={0: 0},
        compiler_params=_params(1),
        name="ffn",
    )(h, g, wg, wu, wd)


def _proj_res_kernel(h_ref, x_ref, w_ref, o_ref):
    o_ref[...] = h_ref[...] + _dot(x_ref[...].astype(BF16), w_ref[...])


def _proj_res(h, x, w, layer):
    m, d = h.shape
    tm = _row_tile(m)
    row = pl.BlockSpec((tm, d), lambda i: (i, 0))
    return pl.pallas_call(
        _proj_res_kernel,
        grid=(m // tm,),
        in_specs=[row, pl.BlockSpec((tm, x.shape[1]), lambda i: (i, 0)), _layer_spec(w, layer)],
        out_specs=row,
        out_shape=jax.ShapeDtypeStruct((m, d), F32),
        input_output_aliases={0: 0},
        compiler_params=_params(1),
        name="proj_res",
    )(h, x, w)


def _norm_glu_kernel(h_ref, g_ref, w_ref, o_ref):
    d = h_ref.shape[1]
    n = _rms(h_ref[...], g_ref[...]).astype(BF16)
    a = _dot(n, w_ref[:, :d])
    gate = _dot(n, w_ref[:, d:])
    o_ref[...] = a * _sigmoid(gate)


def _norm_glu(h, g, w, layer):
    m, d = h.shape
    tm = _row_tile(m)
    row = pl.BlockSpec((tm, d), lambda i: (i, 0))
    return pl.pallas_call(
        _norm_glu_kernel,
        grid=(m // tm,),
        in_specs=[row, _const_spec((1, d)), _layer_spec(w, layer)],
        out_specs=row,
        out_shape=jax.ShapeDtypeStruct((m, d), F32),
        compiler_params=_params(1),
        name="norm_glu",
    )(h, g, w)


def _qkv_kernel(h_ref, g_ref, w_ref, gq_ref, gk_ref, hm_ref, q_ref, k_ref, v_ref, *, hd):
    d = h_ref.shape[1]
    n = _rms(h_ref[...], g_ref[...]).astype(BF16)
    hm = hm_ref[...]

    def head_norm(x, gain):
        x2 = (x * x).astype(BF16)
        ms = jnp.concatenate(
            [_dot(x2[:, c * LANES:(c + 1) * LANES], hm) for c in range(d // LANES)], axis=1)
        return x * lax.rsqrt(ms * (1.0 / hd) + EPS) * gain

    q = head_norm(_dot(n, w_ref[:, :d]), gq_ref[...])
    q_ref[...] = q * (hd ** -0.5)
    k_ref[...] = head_norm(_dot(n, w_ref[:, d:2 * d]), gk_ref[...])
    v_ref[...] = _dot(n, w_ref[:, 2 * d:])


def _qkv(h, g, w, layer, gq, gk, hm, hd):
    m, d = h.shape
    tm = _row_tile(m)
    row = pl.BlockSpec((tm, d), lambda i: (i, 0))
    out = jax.ShapeDtypeStruct((m, d), F32)
    return pl.pallas_call(
        functools.partial(_qkv_kernel, hd=hd),
        grid=(m // tm,),
        in_specs=[row, _const_spec((1, d)), _layer_spec(w, layer), _const_spec((1, d)),
                  _const_spec((1, d)), _const_spec(hm.shape)],
        out_specs=[row, row, row],
        out_shape=[out, out, out],
        compiler_params=_params(1),
        name="qkv",
    )(h, g, w, gq, gk, hm)


def _softplus(z):
    return jnp.maximum(z, 0.0) + jnp.log(1.0 + jnp.exp(-jnp.abs(z)))


def _hi_lo(x):
    hi = x.astype(BF16)
    lo = (x - hi.astype(F32)).astype(BF16)
    return jnp.concatenate([hi, lo], axis=1)


def _suffix_matrix(n, parts):
    i = jnp.arange(n)
    tri = i[:, None] >= i[None, :]
    return jnp.concatenate([tri] * parts, axis=0).astype(BF16)


def _attn_prompt_kernel(bias_ref, q_ref, k_ref, v_ref, tri_ref, o_ref, k16, vm16, carry_ref,
                        acc_ref, z_ref, sp_ref, *, hd, tk):
    hg = pl.program_id(1)
    qi = pl.program_id(2)
    tq = q_ref.shape[0]
    t_pad = k_ref.shape[0]
    nsub = tq // tk
    hpl = LANES // hd
    nlb = q_ref.shape[1] // LANES
    nh = nlb * hpl
    lane = lax.broadcasted_iota(jnp.int32, (tq, LANES), 1)

    @pl.when(qi == 0)
    def _():
        k16[0:t_pad, :] = k_ref[...].astype(BF16)
        v = v_ref[...]
        lanes = lax.broadcasted_iota(jnp.int32, v.shape, 1) % LANES
        tail = k16.shape[0] - t_pad
        for j in range(hpl):
            vm16[j, 0:t_pad, :] = jnp.where(lanes // hd == j, v, 0.0).astype(BF16)
            if tail:
                vm16[j, t_pad:, :] = jnp.zeros((tail, k16.shape[1]), BF16)
        if tail:
            k16[t_pad:, :] = jnp.zeros((tail, k16.shape[1]), BF16)

    tri = tri_ref[...]
    qm = []
    for p in range(nlb):
        qp = q_ref[:, p * LANES:(p + 1) * LANES]
        qm.append(jnp.concatenate([jnp.where(lane // hd == j, qp, 0.0) for j in range(hpl)],
                                  axis=0).astype(BF16))
    bias = [[bias_ref[(hg * nlb + p) * hpl + j] for j in range(hpl)] for p in range(nlb)]
    carry_ref[...] = jnp.zeros(carry_ref.shape, F32)
    acc_ref[...] = jnp.zeros(acc_ref.shape, F32)

    def diagonal_tile(s):
        k0 = pl.multiple_of((qi * nsub + s) * tk, tk)
        lo = s * tk
        mr = tq - lo
        qpos = qi * tq + lo + lax.broadcasted_iota(jnp.int32, (mr, tk), 0)
        mask = k0 + lax.broadcasted_iota(jnp.int32, (mr, tk), 1) < qpos
        zs, sps = [], []
        for p in range(nlb):
            qrows = jnp.concatenate([qm[p][j * tq + lo:(j + 1) * tq] for j in range(hpl)], axis=0)
            zp = _dot_t(qrows, k16[pl.ds(k0, tk), p * LANES:(p + 1) * LANES])
            for j in range(hpl):
                z = zp[j * mr:(j + 1) * mr] + bias[p][j]
                zs.append(z)
                sps.append(jnp.where(mask, _softplus(z), 0.0))
        suffix = _dot(jnp.concatenate([sp.astype(BF16) for sp in sps], axis=0), tri)
        for p in range(nlb):
            ws = []
            for j in range(hpl):
                i = p * hpl + j
                carry = carry_ref[i, lo:, :]
                w = jnp.exp(zs[i] - suffix[i * mr:(i + 1) * mr] - carry)
                ws.append(jnp.where(mask, w, 0.0).astype(BF16))
                carry_ref[i, lo:, :] = carry + jnp.sum(sps[i], axis=1, keepdims=True)
            vm = jnp.concatenate(
                [vm16[j, pl.ds(k0, tk), p * LANES:(p + 1) * LANES] for j in range(hpl)], axis=0)
            acc_ref[p, lo:, :] += _dot(jnp.concatenate(ws, axis=1), vm)

    for s in reversed(range(nsub)):
        diagonal_tile(s)

    def logits(kt, slot):
        k0 = pl.multiple_of(kt * tk, tk)
        for p in range(nlb):
            zp = _dot_t(qm[p], k16[pl.ds(k0, tk), p * LANES:(p + 1) * LANES])
            for j in range(hpl):
                z = zp[j * tq:(j + 1) * tq] + bias[p][j]
                z_ref[slot, p * hpl + j] = z
                sp_ref[slot, p * hpl + j] = _softplus(z)

    def weigh(kt, slot):
        k0 = pl.multiple_of(kt * tk, tk)
        sps = [sp_ref[slot, i] for i in range(nh)]
        suffix = _dot(jnp.concatenate([sp.astype(BF16) for sp in sps], axis=0), tri)
        for p in range(nlb):
            ws = []
            for j in range(hpl):
                i = p * hpl + j
                carry = carry_ref[i]
                w = jnp.exp(z_ref[slot, i] - suffix[i * tq:(i + 1) * tq] - carry)
                ws.append(w.astype(BF16))
                carry_ref[i] = carry + jnp.sum(sps[i], axis=1, keepdims=True)
            vm = jnp.concatenate(
                [vm16[j, pl.ds(k0, tk), p * LANES:(p + 1) * LANES] for j in range(hpl)], axis=0)
            acc_ref[p] += _dot(jnp.concatenate(ws, axis=1), vm)

    n = qi * nsub

    @pl.when(n > 0)
    def _():
        logits(n - 1, 0)

    def body(i, c):
        kt = n - 1 - i
        weigh(kt, i % 2)
        logits(kt - 1, (i + 1) % 2)
        return c

    lax.fori_loop(0, n - 1, body, 0)

    @pl.when(n > 0)
    def _():
        weigh(0, (n - 1) % 2)

    o_ref[...] = jnp.concatenate([acc_ref[p] for p in range(nlb)], axis=1)


def _attn_prompt(q, k, v, bias, tri, hd):
    nb, t_pad, d = q.shape
    tk = SEQ_TILE
    tq = Q_TILE
    nqt = pl.cdiv(t_pad, tq)
    hpl = LANES // hd
    wg = min(d, 4 * LANES)
    assert d % wg == 0 and tq % tk == 0 and t_pad % SEQ_PAD == 0
    seq = pl.BlockSpec((None, t_pad, wg), lambda b, hg, qi: (b, 0, hg))
    tile = pl.BlockSpec((None, tq, wg), lambda b, hg, qi: (b, qi, hg))
    return pl.pallas_call(
        functools.partial(_attn_prompt_kernel, hd=hd, tk=tk),
        grid=(nb, d // wg, nqt),
        in_specs=[pl.BlockSpec(memory_space=pltpu.SMEM), tile, seq, seq, _const_spec(tri.shape)],
        out_specs=tile,
        out_shape=jax.ShapeDtypeStruct((nb, t_pad, d), F32),
        scratch_shapes=[pltpu.VMEM((nqt * tq, wg), BF16), pltpu.VMEM((hpl, nqt * tq, wg), BF16),
                        pltpu.VMEM((wg // hd, tq, tk), F32),
                        pltpu.VMEM((wg // LANES, tq, LANES), F32),
                        pltpu.VMEM((2, wg // hd, tq, tk), F32),
                        pltpu.VMEM((2, wg // hd, tq, tk), F32)],
        compiler_params=_params(3),
        name="attn_prompt",
    )(bias, q, k, v, tri)


def _attn_sample_kernel(pt_ref, q_ref, kn_ref, vn_ref, *refs, nh, hd, pps, n_steps):
    del pt_ref
    kc_refs, vc_refs = refs[:pps], refs[pps:2 * pps]
    bias_ref, tri_ref, o_ref, qbd_ref, carry_ref, acc_ref = refs[2 * pps:]
    s = pl.program_id(1)
    tq, d = q_ref.shape
    tk = kc_refs[0].shape[1]
    r = nh * tq
    hpl = LANES // hd
    rp = hpl * tq
    nlb = d // LANES
    kc = min(d, 2 * LANES)
    rk = kc // hd * tq
    tri = tri_ref[...]
    bias = bias_ref[...]

    @pl.when(s == 0)
    def _():
        row_h = lax.broadcasted_iota(jnp.int32, (r, d), 0) // tq
        lane_h = lax.broadcasted_iota(jnp.int32, (r, d), 1) // hd
        qbd = jnp.where(row_h == lane_h, jnp.tile(q_ref[...], (nh, 1)), 0.0).astype(BF16)
        qbd_ref[...] = qbd
        pad = jnp.zeros((tk - tq, d), F32)
        kb = jnp.concatenate([kn_ref[...], pad], axis=0).astype(BF16)
        vb = jnp.concatenate([vn_ref[...], pad], axis=0).astype(BF16)
        qpos = lax.broadcasted_iota(jnp.int32, (r, tk), 0) % tq
        kpos = lax.broadcasted_iota(jnp.int32, (r, tk), 1)
        mask = kpos < qpos
        z = _dot_t(qbd, kb) + bias
        sp = jnp.where(mask, _softplus(z), 0.0)
        w = jnp.where(mask, jnp.exp(z - _dot(_hi_lo(sp), tri)), 0.0).astype(BF16)
        carry_ref[...] = jnp.zeros((r, tk), F32) + jnp.sum(sp, axis=1, keepdims=True)
        pv = _dot(w, vb)
        acc_ref[...] = jnp.concatenate(
            [pv[c * rp:(c + 1) * rp, c * LANES:(c + 1) * LANES] for c in range(nlb)], axis=0)

    kcat = jnp.concatenate([kc_refs[i][...].astype(BF16) for i in range(pps)], axis=1)
    zcat = jnp.concatenate(
        [_dot(qbd_ref[c * rk:(c + 1) * rk, c * kc:(c + 1) * kc], kcat[c * kc:(c + 1) * kc, :])
         for c in range(d // kc)], axis=0)
    zs = [zcat[:, i * tk:(i + 1) * tk] + bias for i in range(pps)]
    sps = [_softplus(z) for z in zs]
    suffix = _dot(jnp.concatenate([_hi_lo(sp) for sp in sps], axis=0), tri)
    carry = carry_ref[...]
    ws = []
    for i in range(pps):
        ws.append(jnp.exp(zs[i] - suffix[i * r:(i + 1) * r] - carry).astype(BF16))
        carry = carry + jnp.sum(sps[i], axis=1, keepdims=True)
    carry_ref[...] = carry
    wcat = jnp.concatenate(ws, axis=1)
    vcat = jnp.concatenate([vc_refs[i][...].astype(BF16) for i in range(pps)], axis=1)
    acc_ref[...] += jnp.concatenate(
        [_dot_t(wcat[c * rp:(c + 1) * rp, :], vcat[c * LANES:(c + 1) * LANES, :])
         for c in range(nlb)], axis=0)

    @pl.when(s == n_steps - 1)
    def _():
        lane_h = lax.broadcasted_iota(jnp.int32, (tq, LANES), 1) // hd
        cols = []
        for c in range(nlb):
            o = acc_ref[c * rp:c * rp + tq, :]
            for j in range(1, hpl):
                o = jnp.where(lane_h == j, acc_ref[c * rp + j * tq:c * rp + (j + 1) * tq, :], o)
            cols.append(o)
        o_ref[...] = jnp.concatenate(cols, axis=1)


def _attn_sample(q, k, v, cache_kt, cache_vt, layer, page_table, bias_rows, tri, tq, nh, hd):
    m, d = q.shape
    nb, n_pages = page_table.shape
    tk = cache_kt.shape[3]
    r = nh * tq
    pps = max(p for p in range(1, 9) if n_pages % p == 0)
    n_steps = n_pages // pps
    new_rows = pl.BlockSpec((tq, d), lambda b, s, pt: (b, 0))

    def page_spec(i):
        def page_map(b, s, pt):
            return (layer, pt[b * n_pages + n_pages - 1 - (s * pps + i)], 0, 0)
        return pl.BlockSpec((None, None, d, tk), page_map)

    pages = [page_spec(i) for i in range(pps)]
    grid_spec = pltpu.PrefetchScalarGridSpec(
        num_scalar_prefetch=1,
        grid=(nb, n_steps),
        in_specs=[new_rows, new_rows, new_rows] + pages + pages
        + [_const_spec(bias_rows.shape), _const_spec(tri.shape)],
        out_specs=new_rows,
        scratch_shapes=[pltpu.VMEM((r, d), BF16), pltpu.VMEM((r, tk), F32),
                        pltpu.VMEM((r, LANES), F32)],
    )
    return pl.pallas_call(
        functools.partial(_attn_sample_kernel, nh=nh, hd=hd, pps=pps, n_steps=n_steps),
        grid_spec=grid_spec,
        out_shape=jax.ShapeDtypeStruct((m, d), F32),
        compiler_params=_params(2),
        name="attn_sample",
    )(page_table.reshape(-1), q, k, v, *([cache_kt] * pps), *([cache_vt] * pps), bias_rows, tri)


def _kv_out_kernel(*refs):
    o_ref = refs[-1]
    layer = pl.program_id(0)
    x = refs[0][...]
    for i in range(1, len(refs) - 1):
        x = jnp.where(layer == i, refs[i][...], x)
    o_ref[...] = x.T


def _kv_out(xs, t_valid, nh, hd):
    nb, t_pad, d = xs[0].shape
    ts = SEQ_TILE

    def in_spec(i):
        return pl.BlockSpec((None, ts, d), lambda l, b, t: (b, jnp.where(l == i, t, 0), 0))

    out = pl.pallas_call(
        _kv_out_kernel,
        grid=(len(xs), nb, pl.cdiv(t_valid, ts)),
        in_specs=[in_spec(i) for i in range(len(xs))],
        out_specs=pl.BlockSpec((None, None, d, ts), lambda l, b, t: (l, b, 0, t)),
        out_shape=jax.ShapeDtypeStruct((len(xs), nb, d, t_valid), F32),
        compiler_params=_params(3),
        name="kv_out",
    )(*xs)
    return out.reshape(len(xs), nb, nh, hd, t_valid).transpose(0, 1, 4, 2, 3)


def _conv_prompt_kernel(halo_ref, u_ref, h_ref, wdw_ref, gn_ref, w2_ref, o_ref, ext_ref, y_ref,
                        *, nb, rc):
    i = pl.program_id(0)
    rows = u_ref.shape[0]
    hrows = halo_ref.shape[0]
    width = wdw_ref.shape[0]
    ext_ref[0:hrows, :] = jnp.where(i == 0, 0.0, halo_ref[...])
    ext_ref[hrows:, :] = u_ref[...]

    def chunk(c, carry):
        r0 = pl.multiple_of(c * rc, rc)
        y = jnp.zeros((rc, u_ref.shape[1]), F32)
        for j in range(width):
            start = hrows - (width - 1 - j) * nb
            x = ext_ref[pl.ds(pl.multiple_of(r0 + start, SUBLANES), rc), :]
            y = y + x * wdw_ref[j:j + 1, :]
        y_ref[pl.ds(r0, rc), :] = y
        return carry

    lax.fori_loop(0, rows // rc, chunk, 0)
    yn = _rms(y_ref[...], gn_ref[...])
    o_ref[...] = h_ref[...] + _dot((yn * _sigmoid(yn)).astype(BF16), w2_ref[...])


def _conv_prompt(h, u, wdw, gn, w2, layer, nb):
    m, d = h.shape
    assert nb % SUBLANES == 0, "a time shift must move whole sublane groups"
    need = (wdw.shape[0] - 1) * nb
    hrows = next((r for r in range(_round_up(need, LANES), ROW_TILE + 1, LANES) if m % r == 0),
                 need)
    while m % hrows:
        hrows += SUBLANES
    per = max(k for k in range(1, max(ROW_TILE // hrows, 1) + 1) if m % (k * hrows) == 0)
    tm = per * hrows
    rc = _largest_tile(tm, 4 * SUBLANES)
    tile = pl.BlockSpec((tm, d), lambda i: (i, 0))
    halo = pl.BlockSpec((hrows, d), lambda i: (jnp.maximum(i * per - 1, 0), 0))
    return pl.pallas_call(
        functools.partial(_conv_prompt_kernel, nb=nb, rc=rc),
        grid=(m // tm,),
        in_specs=[halo, tile, tile, _const_spec(wdw.shape), _const_spec((1, d)),
                  _layer_spec(w2, layer)],
        out_specs=tile,
        out_shape=jax.ShapeDtypeStruct((m, d), F32),
        input_output_aliases={2: 0},
        scratch_shapes=[pltpu.VMEM((hrows + tm, d), F32), pltpu.VMEM((tm, d), F32)],
        compiler_params=_params(1),
        name="conv_prompt",
    )(u, u, h, wdw, gn, w2)


def _conv_sample_kernel(st_ref, u_ref, h_ref, wdw_ref, gn_ref, w2_ref, o_ref):
    width = wdw_ref.shape[0]
    ns = st_ref.shape[0]
    for t in range(u_ref.shape[0]):
        y = jnp.zeros(h_ref.shape[1:], F32)
        for j in range(width):
            i = t + j
            x = st_ref[i] if i < ns else u_ref[i - ns]
            y = y + x * wdw_ref[j:j + 1, :]
        yn = _rms(y, gn_ref[...])
        o_ref[t] = h_ref[t] + _dot((yn * _sigmoid(yn)).astype(BF16), w2_ref[...])


def _conv_sample(h_t, u_t, st_t, wdw, gn, w2, layer):
    t, nb, d = h_t.shape
    bs = _largest_tile(nb, 32)
    blk = lambda n: pl.BlockSpec((n, bs, d), lambda i: (0, i, 0))
    return pl.pallas_call(
        _conv_sample_kernel,
        grid=(nb // bs,),
        in_specs=[blk(st_t.shape[0]), blk(t), blk(t), _const_spec(wdw.shape),
                  _const_spec((1, d)), _layer_spec(w2, layer)],
        out_specs=blk(t),
        out_shape=jax.ShapeDtypeStruct((t, nb, d), F32),
        compiler_params=_params(1),
        name="conv_sample",
    )(st_t, u_t, h_t, wdw, gn, w2)


def _ssm_prep_kernel(are_ref, aim_ref, ldt_ref, bre_ref, bim_ref,
                     pwre_ref, pwim_ref, bbre_ref, bbim_ref):
    a_re, a_im = are_ref[...], aim_ref[...]
    dt = jnp.exp(ldt_ref[...])
    mag = jnp.exp(dt * a_re)
    ab_re, ab_im = mag * jnp.cos(dt * a_im), mag * jnp.sin(dt * a_im)
    den = a_re * a_re + a_im * a_im
    nr, ni = ab_re - 1.0, ab_im
    f_re = (nr * a_re + ni * a_im) / den
    f_im = (ni * a_re - nr * a_im) / den
    b_re, b_im = bre_ref[...], bim_ref[...]
    bbre_ref[...] = f_re * b_re - f_im * b_im
    bbim_ref[...] = f_re * b_im + f_im * b_re
    p_re, p_im = ab_re, ab_im
    for k in range(pwre_ref.shape[0]):
        pwre_ref[k:k + 1, :] = p_re
        pwim_ref[k:k + 1, :] = p_im
        p_re, p_im = p_re * ab_re - p_im * ab_im, p_re * ab_im + p_im * ab_re


def _ssm_prep(a_re, a_im, ldt, b_re, b_im):
    gp = a_re.shape[1]
    gc = b_re.shape[0]
    vec = jax.ShapeDtypeStruct((SUBLANES, gp), F32)
    mat = jax.ShapeDtypeStruct((gc, gp), F32)
    return pl.pallas_call(
        _ssm_prep_kernel,
        out_shape=[vec, vec, mat, mat],
        compiler_params=pltpu.CompilerParams(vmem_limit_bytes=VMEM_LIMIT),
        name="ssm_prep",
    )(a_re, a_im, ldt, b_re, b_im)


def _ssm_kernel(*refs, nb, t_last, lc):
    chained = nb > 0
    if chained:
        (h_ref, g_ref, bbre_ref, bbim_ref, ccre_ref, ccim_ref, pwre_ref, pwim_ref, dsk_ref,
         wglu_ref, o_ref, stre_ref, stim_ref, sre, sim, cre, cim) = refs
    else:
        (h_ref, g_ref, bbre_ref, bbim_ref, ccre_ref, ccim_ref, pwre_ref, pwim_ref, dsk_ref,
         wglu_ref, h0re_ref, h0im_ref, o_ref, stre_ref, stim_ref, sre, sim) = refs
    rows, d = h_ref.shape
    gp = sre.shape[1]
    nblk = d // LANES
    cw = gp // nblk

    h = h_ref[...]
    n = _rms(h, g_ref[...])
    nb16 = n.astype(BF16)
    for c in range(nblk):
        x = nb16[:, c * LANES:(c + 1) * LANES]
        sre[:, c * cw:(c + 1) * cw] = _dot(x, bbre_ref[c])
        sim[:, c * cw:(c + 1) * cw] = _dot(x, bbim_ref[c])

    if chained:
        step = pl.program_id(0)

        @pl.when(step == 0)
        def _():
            cre[...] = jnp.zeros(cre.shape, F32)
            cim[...] = jnp.zeros(cim.shape, F32)

        for c in range(gp // lc):
            ls = slice(c * lc, (c + 1) * lc)
            a_re = jnp.broadcast_to(pwre_ref[0:1, ls], (nb, lc))
            a_im = jnp.broadcast_to(pwim_ref[0:1, ls], (nb, lc))

            def body(t, carry, ls=ls, a_re=a_re, a_im=a_im):
                x_re, x_im = carry
                r0 = pl.multiple_of(t * nb, nb)
                x_re, x_im = (a_re * x_re - a_im * x_im + sre[pl.ds(r0, nb), ls],
                              a_re * x_im + a_im * x_re + sim[pl.ds(r0, nb), ls])
                sre[pl.ds(r0, nb), ls] = x_re
                sim[pl.ds(r0, nb), ls] = x_im
                return x_re, x_im

            x_re, x_im = lax.fori_loop(0, rows // nb, body, (cre[:, ls], cim[:, ls]), unroll=8)
            cre[:, ls] = x_re
            cim[:, ls] = x_im

        @pl.when(step == t_last * nb // rows)
        def _():
            r0 = t_last * nb % rows
            stre_ref[...] = sre[r0:r0 + nb, :]
            stim_ref[...] = sim[r0:r0 + nb, :]
    else:
        rowi = lax.broadcasted_iota(jnp.int32, (SUBLANES, lc), 0)
        for c in range(gp // lc):
            ls = slice(c * lc, (c + 1) * lc)
            p_re, p_im = pwre_ref[:, ls], pwim_ref[:, ls]

            def body(r, carry, ls=ls, p_re=p_re, p_im=p_im):
                r0 = pl.multiple_of(r * SUBLANES, SUBLANES)
                c_re, c_im = h0re_ref[pl.ds(r, 1), ls], h0im_ref[pl.ds(r, 1), ls]
                x_re, x_im = sre[pl.ds(r0, SUBLANES), ls], sim[pl.ds(r0, SUBLANES), ls]
                for dist in (1, 2, 4):
                    a_re, a_im = p_re[dist - 1:dist], p_im[dist - 1:dist]
                    s_re = jnp.where(rowi >= dist, pltpu.roll(x_re, dist, axis=0), 0.0)
                    s_im = jnp.where(rowi >= dist, pltpu.roll(x_im, dist, axis=0), 0.0)
                    x_re, x_im = (x_re + a_re * s_re - a_im * s_im,
                                  x_im + a_re * s_im + a_im * s_re)
                x_re, x_im = (x_re + p_re * c_re - p_im * c_im,
                              x_im + p_re * c_im + p_im * c_re)
                sre[pl.ds(r0, SUBLANES), ls] = x_re
                sim[pl.ds(r0, SUBLANES), ls] = x_im
                stre_ref[pl.ds(r, 1), ls] = x_re[SUBLANES - 1:]
                stim_ref[pl.ds(r, 1), ls] = x_im[SUBLANES - 1:]
                return carry

            lax.fori_loop(0, rows // SUBLANES, body, 0)

    hre = sre[...].astype(BF16)
    him = sim[...].astype(BF16)
    y = jnp.concatenate(
        [_dot(hre[:, c * cw:(c + 1) * cw], ccre_ref[c]) - _dot(him[:, c * cw:(c + 1) * cw], ccim_ref[c])
         for c in range(nblk)], axis=1)
    y = (y + dsk_ref[...] * n).astype(BF16)
    za = _dot(y, wglu_ref[:, :d])
    zb = _dot(y, wglu_ref[:, d:])
    o_ref[...] = h + za * _sigmoid(zb)


def _ssm_common_specs(d, bb, cc, pw, wglu, layer):
    return [_const_spec((1, d)), _const_spec(bb.shape), _const_spec(bb.shape),
            _const_spec(cc.shape), _const_spec(cc.shape), _const_spec(pw.shape),
            _const_spec(pw.shape), _const_spec((1, d)), _layer_spec(wglu, layer)]


def _ssm_prompt(h, g, bbre, bbim, ccre, ccim, pwre, pwim, dsk, wglu, layer, nb, t_valid):
    m, d = h.shape
    gp = pwre.shape[1]
    assert nb % SUBLANES == 0, "one time step of all sequences must fill whole sublane groups"
    rows = _largest_tile(m, 256, nb)
    lc = min(SSM_LANE_CHUNK, gp)
    tile = pl.BlockSpec((rows, d), lambda i: (i, 0))
    st = _const_spec((nb, gp))
    st_shape = jax.ShapeDtypeStruct((nb, gp), F32)
    return pl.pallas_call(
        functools.partial(_ssm_kernel, nb=nb, t_last=t_valid - 1, lc=lc),
        grid=(m // rows,),
        in_specs=[tile] + _ssm_common_specs(d, bbre, ccre, pwre, wglu, layer),
        out_specs=[tile, pl.BlockSpec((nb, gp), lambda i: (0, 0)),
                   pl.BlockSpec((nb, gp), lambda i: (0, 0))],
        out_shape=[jax.ShapeDtypeStruct((m, d), F32), st_shape, st_shape],
        input_output_aliases={0: 0},
        scratch_shapes=[pltpu.VMEM((rows, gp), F32), pltpu.VMEM((rows, gp), F32),
                        pltpu.VMEM((nb, gp), F32), pltpu.VMEM((nb, gp), F32)],
        compiler_params=_params(1),
        name="ssm_prompt",
    )(h, g, bbre, bbim, ccre, ccim, pwre, pwim, dsk, wglu)


def _ssm_sample(h, g, bbre, bbim, ccre, ccim, pwre, pwim, dsk, wglu, layer, h0re, h0im, t):
    m, d = h.shape
    assert t == SUBLANES, "the sample scan handles one 8-token block per sequence"
    nb, gp = h0re.shape
    bs = _largest_tile(nb, 32)
    rows = bs * t
    lc = min(SSM_LANE_CHUNK, gp)
    tile = pl.BlockSpec((rows, d), lambda i: (i, 0))
    st = pl.BlockSpec((bs, gp), lambda i: (i, 0))
    st_shape = jax.ShapeDtypeStruct((nb, gp), F32)
    return pl.pallas_call(
        functools.partial(_ssm_kernel, nb=0, t_last=0, lc=lc),
        grid=(nb // bs,),
        in_specs=[tile] + _ssm_common_specs(d, bbre, ccre, pwre, wglu, layer) + [st, st],
        out_specs=[tile, st, st],
        out_shape=[jax.ShapeDtypeStruct((m, d), F32), st_shape, st_shape],
        input_output_aliases={0: 0},
        scratch_shapes=[pltpu.VMEM((rows, gp), F32), pltpu.VMEM((rows, gp), F32)],
        compiler_params=_params(1),
        name="ssm_sample",
    )(h, g, bbre, bbim, ccre, ccim, pwre, pwim, dsk, wglu, h0re, h0im)


def _block_diag_blocks(full, row_group, col_group, row_blk, col_blk):
    rows, cols = full.shape
    keep = (jnp.arange(rows)[:, None] // row_group) == (jnp.arange(cols)[None, :] // col_group)
    full = jnp.where(keep, full, 0.0)
    return jnp.stack([full[i * row_blk:(i + 1) * row_blk, i * col_blk:(i + 1) * col_blk]
                      for i in range(rows // row_blk)]).astype(BF16)


def kernel(x_prompt, x_sample, cache_k, cache_v, page_table, state_conv, state_ssm_re, state_ssm_im, meta_tokens, norm_g, ffn1_w_gate, ffn1_w_up, ffn1_w_down, ffn2_w_gate, ffn2_w_up, ffn2_w_down, sb_w_qkv, sb_w_o, sb_q_norm, sb_k_norm, sb_logit_bias, conv_w_pw1, conv_w_dw, conv_norm_g, conv_w_pw2, ssm_A_re, ssm_A_im, ssm_log_dt, ssm_B_re, ssm_B_im, ssm_C_re, ssm_C_im, ssm_D, ssm_w_glu):
    nb, seq, d = x_prompt.shape
    nsb, ts_s, _ = x_sample.shape
    depth = norm_g.shape[0]
    n_meta = meta_tokens.shape[0]
    nh = sb_logit_bias.shape[1]
    hd = d // nh
    t_valid = seq + n_meta
    t_pad = _round_up(t_valid, SEQ_PAD)
    m_p = nb * t_pad
    m_s = nsb * ts_s
    assert d % LANES == 0 and LANES % hd == 0

    meta = jnp.broadcast_to(meta_tokens[None], (nb, n_meta, d))
    pad = jnp.zeros((nb, t_pad - t_valid, d), F32)
    hp = jnp.concatenate([meta, x_prompt, pad], axis=1).reshape(m_p, d)
    hs = x_sample.reshape(m_s, d)
    time_major = False

    def set_order(x, want_time_major):
        if want_time_major == time_major:
            return x
        a, b = (nb, t_pad) if want_time_major else (t_pad, nb)
        return x.reshape(a, b, d).transpose(1, 0, 2).reshape(m_p, d)

    page = cache_k.shape[2]
    cache_kt = cache_k.transpose(0, 1, 3, 4, 2).reshape(cache_k.shape[:2] + (d, page))
    cache_vt = cache_v.transpose(0, 1, 3, 4, 2).reshape(cache_v.shape[:2] + (d, page))
    idx = jnp.arange(LANES)
    head_ones = (idx[:, None] // hd == idx[None, :] // hd).astype(BF16)

    ffn1 = [w.astype(BF16) for w in (ffn1_w_gate, ffn1_w_up, ffn1_w_down)]
    ffn2 = [w.astype(BF16) for w in (ffn2_w_gate, ffn2_w_up, ffn2_w_down)]
    w_qkv, w_o = sb_w_qkv.astype(BF16), sb_w_o.astype(BF16)
    w_pw1, w_pw2 = conv_w_pw1.astype(BF16), conv_w_pw2.astype(BF16)
    w_glu = ssm_w_glu.astype(BF16)

    new_k, new_v, new_conv, new_re, new_im = [], [], [], [], []
    for i in range(depth):
        kind, j = i % 3, i // 3
        g1, g_mix, g2 = norm_g[i, 0][None], norm_g[i, 1][None], norm_g[i, 2][None]
        hp, hs = _ffn(hp, g1, *ffn1, i), _ffn(hs, g1, *ffn1, i)
        if kind == 0:
            hp = set_order(hp, False)
            time_major = False
            gq, gk = jnp.tile(sb_q_norm[j], nh)[None], jnp.tile(sb_k_norm[j], nh)[None]
            qp, kp, vp = [x.reshape(nb, t_pad, d)
                          for x in _qkv(hp, g_mix, w_qkv, j, gq, gk, head_ones, hd)]
            qs, ks, vs = _qkv(hs, g_mix, w_qkv, j, gq, gk, head_ones, hd)
            bias = sb_logit_bias[j]
            op = _attn_prompt(qp, kp, vp, bias, _suffix_matrix(SEQ_TILE, 1), hd)
            bias_rows = jnp.broadcast_to(jnp.repeat(bias, ts_s)[:, None], (nh * ts_s, page))
            os_ = _attn_sample(qs, ks, vs, cache_kt, cache_vt, j, page_table, bias_rows,
                               _suffix_matrix(page, 2), ts_s, nh, hd)
            hp = _proj_res(hp, op.reshape(m_p, d), w_o, j)
            hs = _proj_res(hs, os_, w_o, j)
            new_k.append((kp, ks.reshape(nsb, ts_s, nh, hd)))
            new_v.append((vp, vs.reshape(nsb, ts_s, nh, hd)))
        elif kind == 1:
            hp = set_order(hp, True)
            time_major = True
            wdw, gn = conv_w_dw[j], conv_norm_g[j][None]
            width = wdw.shape[0]
            up = _norm_glu(hp, g_mix, w_pw1, j)
            hp = _conv_prompt(hp, up, wdw, gn, w_pw2, j, nb)
            us = _norm_glu(hs, g_mix, w_pw1, j).reshape(nsb, ts_s, d)
            hs_t = _conv_sample(hs.reshape(nsb, ts_s, d).transpose(1, 0, 2), us.transpose(1, 0, 2),
                                state_conv[j].transpose(1, 0, 2), wdw, gn, w_pw2, j)
            hs = hs_t.transpose(1, 0, 2).reshape(m_s, d)
            up_tail = up.reshape(t_pad, nb, d)[t_valid - (width - 1):t_valid]
            new_conv.append((up_tail.transpose(1, 0, 2),
                             jnp.concatenate([state_conv[j], us], axis=1)[:, -(width - 1):]))
        else:
            hp = set_order(hp, True)
            time_major = True
            ng, ns = ssm_A_re.shape[1:]
            gc = ssm_B_re.shape[3]
            gp = ng * ns
            pwre, pwim, bbre, bbim = _ssm_prep(
                ssm_A_re[j].reshape(1, gp), ssm_A_im[j].reshape(1, gp),
                jnp.repeat(ssm_log_dt[j], ns)[None],
                ssm_B_re[j].transpose(2, 0, 1).reshape(gc, gp),
                ssm_B_im[j].transpose(2, 0, 1).reshape(gc, gp))
            gpb = LANES // gc
            bb = [_block_diag_blocks(jnp.tile(x, (ng, 1)), gc, ns, LANES, gpb * ns)
                  for x in (bbre, bbim)]
            cc = [_block_diag_blocks(jnp.tile(x[j].transpose(0, 2, 1).reshape(gp, gc), (1, ng)),
                                     ns, gc, gpb * ns, LANES) for x in (ssm_C_re, ssm_C_im)]
            args = (g_mix, bb[0], bb[1], cc[0], cc[1], pwre, pwim, ssm_D[j][None], w_glu, j)
            hp, pre, pim = _ssm_prompt(hp, *args, nb, t_valid)
            hs, sre, sim = _ssm_sample(hs, *args, state_ssm_re[j].reshape(nsb, gp),
                                       state_ssm_im[j].reshape(nsb, gp), ts_s)
            new_re.append((pre.reshape(nb, ng, ns), sre.reshape(nsb, ng, ns)))
            new_im.append((pim.reshape(nb, ng, ns), sim.reshape(nsb, ng, ns)))
        hp, hs = _ffn(hp, g2, *ffn2, i), _ffn(hs, g2, *ffn2, i)

    def stack(pairs, which):
        return jnp.stack([p[which] for p in pairs])

    hp = set_order(hp, False)
    y_prompt = hp.reshape(nb, t_pad, d)[:, n_meta:t_valid]
    y_sample = hs.reshape(nsb, ts_s, d)
    k_prompt = _kv_out([p[0] for p in new_k], t_valid, nh, hd)
    v_prompt = _kv_out([p[0] for p in new_v], t_valid, nh, hd)
    return (y_prompt, y_sample, k_prompt, v_prompt, stack(new_k, 1),
            stack(new_v, 1), stack(new_conv, 0), stack(new_conv, 1), stack(new_re, 0),
            stack(new_im, 0), stack(new_re, 1), stack(new_im, 1))
```
